```python
import math
import jax, jax.numpy as jnp
from jax import lax
import numpy as np

D_MODEL = 1024
BATCH = 8
SEQ = 2048
DEPTH = 1
DEC_BATCH = 128
DEC_SEQ = 1
PAST_LEN = 16384
PAGE_SIZE = 128

MIX_W = D_MODEL
CONV_W = MIX_W // 2
CONV_GROUPS = 4
CONV_K = 3
RET_W = MIX_W // 2
RET_HEADS = 4
RET_HD = RET_W // RET_HEADS
RET_CHUNK = 128
ROPE_BASE = 10000.0
PROJ_W = 3 * CONV_W + 4 * RET_W
N_EXPERTS = 32
TOP_K = 4
D_FF = D_MODEL
SWIGLU_LIMIT = 7.0
SWIGLU_ALPHA = 1.702
EXPERT_BLOCK = 128
RMS_EPS = 1e-6
GN_EPS = 1e-5

kernel_name = "hymba_conv_retention_moe_step"


def rmsnorm(x, g):
    xf = x.astype(jnp.float32)
    y = xf * lax.rsqrt(jnp.mean(xf * xf, axis=-1, keepdims=True) + RMS_EPS)
    return (y * g.astype(jnp.float32)).astype(x.dtype)


def rotary(x, positions):
    half = x.shape[-1] // 2
    inv = ROPE_BASE ** (-jnp.arange(half, dtype=jnp.float32) / half)
    ang = positions.astype(jnp.float32)[:, None] * inv[None, :]
    cos, sin = jnp.cos(ang), jnp.sin(ang)
    x1, x2 = x[..., :half], x[..., half:]
    return jnp.concatenate([x1 * cos - x2 * sin, x1 * sin + x2 * cos], axis=-1)


def retention_chunk(S, q, k, v, log_gamma):
    c = q.shape[2]
    pos = jnp.arange(c, dtype=jnp.float32)
    diff = pos[:, None] - pos[None, :]
    causal = diff >= 0
    lg = log_gamma[:, None, None]
    decay = jnp.where(causal, jnp.exp(lg * jnp.where(causal, diff, 0.0)), 0.0)
    scores = jnp.einsum('bhqd,bhkd->bhqk', q, k) * decay[None]
    inner = jnp.einsum('bhqk,bhkd->bhqd', scores, v)
    xi = jnp.exp(log_gamma[:, None] * (pos[None, :] + 1.0))
    cross = jnp.einsum('bhqd,bhde->bhqe', q * xi[None, :, :, None], S)
    zeta = jnp.exp(log_gamma[:, None] * (c - 1.0 - pos[None, :]))
    S_new = jnp.exp(log_gamma * c)[None, :, None, None] * S + jnp.einsum(
        'bhkd,bhke->bhde', k * zeta[None, :, :, None], v)
    return S_new, inner + cross


def retention(q, k, v, S0, log_gamma):
    B, H, L, hd = q.shape
    c = min(RET_CHUNK, L)
    n = L // c

    def to_chunks(t):
        return t.reshape(B, H, n, c, hd).transpose(2, 0, 1, 3, 4)

    S, o = lax.scan(lambda s, qkv: retention_chunk(s, qkv[0], qkv[1], qkv[2], log_gamma),
                    S0, (to_chunks(q), to_chunks(k), to_chunks(v)))
    o = o.transpose(1, 2, 0, 3, 4).reshape(B, H, L, hd)
    return o, S


def token_mixer(xn, pos_offset, conv_state, ret_state, w_in, conv_w, ret_norm, w_out):
    B, L, _ = xn.shape
    proj = xn @ w_in
    cuts = [CONV_W, 2 * CONV_W, 3 * CONV_W, 3 * CONV_W + RET_W,
            3 * CONV_W + 2 * RET_W, 3 * CONV_W + 3 * RET_W]
    xc, gb, gc, q, k, v, g = jnp.split(proj, cuts, axis=-1)

    u = gc * xc
    buf = jnp.concatenate([conv_state.astype(u.dtype), u], axis=1)
    conv_out = sum(conv_w[j] * buf[:, j:j + L] for j in range(CONV_K))
    a_out = gb * conv_out
    new_conv = buf[:, L:]

    positions = pos_offset + jnp.arange(L, dtype=jnp.int32)

    def heads(t):
        return t.astype(jnp.float32).reshape(B, L, RET_HEADS, RET_HD).transpose(0, 2, 1, 3)

    qh = rotary(heads(q), positions)
    kh = rotary(heads(k), positions) * (RET_HD ** -0.5)
    vh = heads(v)
    log_gamma = jnp.log(1.0 - 2.0 ** (-5.0 - jnp.arange(RET_HEADS, dtype=jnp.float32)))
    o, new_ret = retention(qh, kh, vh, ret_state.astype(jnp.float32), log_gamma)
    mu = jnp.mean(o, axis=-1, keepdims=True)
    var = jnp.mean((o - mu) ** 2, axis=-1, keepdims=True)
    o = (o - mu) * lax.rsqrt(var + GN_EPS)
    o = o * ret_norm.astype(jnp.float32).reshape(RET_HEADS, RET_HD)[None, :, None, :]
    o = o.transpose(0, 2, 1, 3).reshape(B, L, RET_W).astype(xn.dtype)
    r_out = jax.nn.silu(g) * o

    y = jnp.concatenate([a_out, r_out], axis=-1) @ w_out
    return y, new_conv, new_ret


def moe(xn, w_router, b_router, w_up, b_up, w_down, b_down):
    Bt, L, D = xn.shape
    x = xn.reshape(-1, D)
    T = x.shape[0]
    logits = (x @ w_router + b_router).astype(jnp.float32)
    top_vals, top_idx = lax.top_k(logits, TOP_K)
    gates = jax.nn.softmax(top_vals, axis=-1).astype(x.dtype)

    N = T * TOP_K
    flat_e = top_idx.reshape(-1).astype(jnp.int32)
    order = jnp.argsort(flat_e).astype(jnp.int32)
    sorted_e = flat_e[order]
    counts = jnp.bincount(flat_e, length=N_EXPERTS).astype(jnp.int32)
    padded = (counts + EXPERT_BLOCK - 1) // EXPERT_BLOCK * EXPERT_BLOCK
    start = jnp.cumsum(counts) - counts
    pend = jnp.cumsum(padded)
    pstart = pend - padded
    dest = pstart[sorted_e] + (jnp.arange(N, dtype=jnp.int32) - start[sorted_e])
    n_blocks = -(-N // EXPERT_BLOCK) + N_EXPERTS
    R = n_blocks * EXPERT_BLOCK
    row_token = jnp.full((R,), T, jnp.int32).at[dest].set(order // TOP_K)
    x_pad = jnp.concatenate([x, jnp.zeros((1, D), x.dtype)], axis=0)
    x_rows = x_pad[row_token].reshape(n_blocks, EXPERT_BLOCK, D)
    block_e = jnp.clip(jnp.searchsorted(pend, jnp.arange(n_blocks, dtype=jnp.int32) * EXPERT_BLOCK,
                                        side='right'), 0, N_EXPERTS - 1)

    def expert_block(args):
        xb, e = args
        h = xb @ w_up[e] + b_up[e]
        h_glu = jnp.minimum(h[:, :D_FF], SWIGLU_LIMIT)
        h_lin = jnp.clip(h[:, D_FF:], -SWIGLU_LIMIT, SWIGLU_LIMIT)
        act = h_glu * jax.nn.sigmoid(SWIGLU_ALPHA * h_glu) * (h_lin + 1.0)
        return act @ w_down[e] + b_down[e]

    y_rows = lax.map(expert_block, (x_rows, block_e)).reshape(R, D)
    dest_of_assign = jnp.zeros((N,), jnp.int32).at[order].set(dest)
    y_assign = y_rows[dest_of_assign].reshape(T, TOP_K, D)
    y = jnp.einsum('tk,tkd->td', gates, y_assign)
    return y.reshape(Bt, L, D)


def decoder_layer(x, pos_offset, conv_state, ret_state, norm_mix, w_in, conv_w, ret_norm,
                  w_out, norm_ffn, w_router, b_router, w_up, b_up, w_down, b_down):
    mix, new_conv, new_ret = token_mixer(rmsnorm(x, norm_mix), pos_offset, conv_state, ret_state,
                                         w_in, conv_w, ret_norm, w_out)
    h = x + mix
    h = h + moe(rmsnorm(h, norm_ffn), w_router, b_router, w_up, b_up, w_down, b_down)
    return h, new_conv, new_ret


def setup_inputs(seed: int = 0) -> dict:
    key = jax.random.key(seed)
    ks = jax.random.split(key, 20)
    f32 = jnp.float32
    nrm = lambda k, s, sc: jax.random.normal(k, s, f32) * sc
    return {
        "x_prompt": nrm(ks[0], (BATCH, SEQ, D_MODEL), 1.0),
        "x_sample": nrm(ks[1], (DEC_BATCH, DEC_SEQ, D_MODEL), 1.0),
        "state_conv": nrm(ks[2], (DEPTH, DEC_BATCH, CONV_K - 1, CONV_W), 1.0),
        "state_ret": nrm(ks[3], (DEPTH, DEC_BATCH, RET_HEADS, RET_HD, RET_HD), 0.5),
        "norm_mix": 1.0 + nrm(ks[4], (DEPTH, D_MODEL), 0.02),
        "w_in": nrm(ks[5], (DEPTH, D_MODEL, PROJ_W), D_MODEL ** -0.5),
        "conv_w": nrm(ks[6], (DEPTH, CONV_K, CONV_W), CONV_K ** -0.5),
        "ret_norm": 1.0 + nrm(ks[7], (DEPTH, RET_W), 0.02),
        "w_out": nrm(ks[8], (DEPTH, MIX_W, D_MODEL), MIX_W ** -0.5),
        "norm_ffn": 1.0 + nrm(ks[9], (DEPTH, D_MODEL), 0.02),
        "w_router": nrm(ks[10], (DEPTH, D_MODEL, N_EXPERTS), D_MODEL ** -0.5),
        "b_router": nrm(ks[11], (DEPTH, N_EXPERTS), 0.01),
        "w_up": nrm(ks[12], (DEPTH, N_EXPERTS, D_MODEL, 2 * D_FF), D_MODEL ** -0.5),
        "b_up": nrm(ks[13], (DEPTH, N_EXPERTS, 2 * D_FF), 0.01),
        "w_down": nrm(ks[14], (DEPTH, N_EXPERTS, D_FF, D_MODEL), D_FF ** -0.5),
        "b_down": nrm(ks[15], (DEPTH, N_EXPERTS, D_MODEL), 0.01),
        "norm_final": 1.0 + nrm(ks[16], (D_MODEL,), 0.02),
    }


def reference(x_prompt, x_sample, state_conv, state_ret, norm_mix, w_in, conv_w, ret_norm,
              w_out, norm_ffn, w_router, b_router, w_up, b_up, w_down, b_down, norm_final):
    hp, hs = x_prompt, x_sample
    conv_p, ret_p, conv_s, ret_s = [], [], [], []
    for l in range(DEPTH):
        params = (norm_mix[l], w_in[l], conv_w[l], ret_norm[l], w_out[l], norm_ffn[l],
                  w_router[l], b_router[l], w_up[l], b_up[l], w_down[l], b_down[l])
        zero_conv = jnp.zeros((hp.shape[0], CONV_K - 1, CONV_W), hp.dtype)
        zero_ret = jnp.zeros((hp.shape[0], RET_HEADS, RET_HD, RET_HD), jnp.float32)
        hp, cp, rp = decoder_layer(hp, 0, zero_conv, zero_ret, *params)
        hs, cs, rs = decoder_layer(hs, PAST_LEN, state_conv[l], state_ret[l], *params)
        conv_p.append(cp); ret_p.append(rp); conv_s.append(cs); ret_s.append(rs)
    y_prompt = rmsnorm(hp, norm_final)
    y_sample = rmsnorm(hs, norm_final)
    return (y_prompt, y_sample, jnp.stack(conv_p), jnp.stack(ret_p), jnp.stack(conv_s), jnp.stack(ret_s))
```

```python
import functools

import jax
import jax.numpy as jnp
from jax import lax
from jax.experimental import pallas as pl
from jax.experimental.pallas import tpu as pltpu

F32 = jnp.float32
BF16 = jnp.bfloat16
I32 = jnp.int32

D_MODEL = 1024
BATCH = 8
SEQ = 2048
DEC_BATCH = 128
PAST_LEN = 16384
CONV_W = 512
CONV_K = 3
RET_W = 512
RET_HEADS = 4
RET_HD = 128
ROPE_BASE = 10000.0
PROJ_W = 3 * CONV_W + 4 * RET_W
N_EXPERTS = 32
TOP_K = 4
D_FF = D_MODEL
SWIGLU_LIMIT = 7.0
SWIGLU_ALPHA = 1.702
RMS_EPS = 1e-6
GN_EPS = 1e-5

T_PROMPT = BATCH * SEQ
T_ALL = T_PROMPT + DEC_BATCH
N_ASSIGN = T_ALL * TOP_K

LANES = 128
SUBLANES = 8
TILE_L = 256
N_TILES = T_PROMPT // TILE_L
T_BUF = (N_TILES + 1) * TILE_L
BLOCK_M = 256
BLOCK_SHIFT = 8
N_BLOCKS = N_ASSIGN // BLOCK_M + N_EXPERTS
BLOCK_LANES = 3 * LANES
DUMP_ROWS = N_EXPERTS * BLOCK_M // TOP_K
SLAB = T_ALL + DUMP_ROWS
PAD_BASE = TOP_K * T_ALL
N_CHUNKS = T_ALL // LANES
DEST_ROWS = 520
FINAL_TILE = 512
VMEM_LIMIT = 56 * 1024 * 1024

assert N_ASSIGN % BLOCK_M == 0 and N_BLOCKS <= BLOCK_LANES and (1 << BLOCK_SHIFT) == BLOCK_M
assert T_ALL % LANES == 0 and TOP_K * N_CHUNKS <= DEST_ROWS


def _rmsnorm(x, g):
    ms = jnp.mean(x * x, axis=-1, keepdims=True)
    return x * lax.rsqrt(ms + RMS_EPS) * g


def _dot(a, b):
    return jnp.dot(a, b, preferred_element_type=F32)


def _dot_nt(a, b):
    return lax.dot_general(a, b, (((1,), (1,)), ((), ())), preferred_element_type=F32)


def _dot_tn(a, b):
    return lax.dot_general(a, b, (((0,), (0,)), ((), ())), preferred_element_type=F32)


def _rotary(x, cos2, sin2):
    return x * cos2 + pltpu.roll(x, RET_HD // 2, 1) * sin2


def _group_norm_gate(o, g, w):
    mu = jnp.mean(o, axis=-1, keepdims=True)
    d = o - mu
    var = jnp.mean(d * d, axis=-1, keepdims=True)
    return (g * jax.nn.sigmoid(g)) * (d * lax.rsqrt(var + GN_EPS) * w)


def _route(hn, wr_hi, wr_lo, b_router, gate_ref, idx_ref):
    rows = hn.shape[0]
    hi = hn.astype(BF16)
    lo = (hn - hi.astype(F32)).astype(BF16)
    logits = _dot(hi, wr_hi) + _dot(lo, wr_hi) + _dot(hi, wr_lo) + b_router
    lane = lax.broadcasted_iota(I32, (rows, N_EXPERTS), 1)
    vals, sels = [], []
    onehot = jnp.zeros((rows, N_EXPERTS), F32)
    for _ in range(TOP_K):
        m = jnp.max(logits, axis=-1, keepdims=True)
        sel = jnp.min(jnp.where(logits == m, lane, N_EXPERTS), axis=-1, keepdims=True)
        hit = lane == sel
        onehot = onehot + hit.astype(F32)
        logits = jnp.where(hit, -jnp.inf, logits)
        vals.append(m)
        sels.append(sel)
    ex = [jnp.exp(v - vals[0]) for v in vals]
    den = ex[0] + ex[1] + ex[2] + ex[3]
    zero = jnp.zeros((rows, SUBLANES - TOP_K), F32)
    gate_ref[...] = jnp.concatenate([e / den for e in ex] + [zero], axis=1)
    idx_ref[...] = jnp.concatenate(sels + [zero.astype(I32)], axis=1)
    return jnp.sum(onehot.reshape(rows // SUBLANES, SUBLANES, N_EXPERTS), axis=0)


def _mixer_prompt_kernel(x_ref, nmix_ref, win_ref, convw_ref, cos_ref, sin_ref, decay_ref, xi_ref,
                         zeta_ref, gl_ref, rnorm_ref, wout_ref, nffn_ref, wrhi_ref, wrlo_ref, br_ref,
                         hs_ref, hns_ref, idxs_ref, gates_ref, cnts_ref,
                         h_ref, hn_ref, idx_ref, gate_ref, cnt_ref, nconv_ref, nret_ref,
                         ubuf, sbuf, mixbuf):
    step = pl.program_id(0)

    @pl.when(step == 0)
    def _():
        cnt_ref[...] = cnts_ref[...]

    @pl.when(step < N_TILES)
    def _():
        _mixer_prompt_tile(step % (SEQ // TILE_L), x_ref, nmix_ref, win_ref, convw_ref, cos_ref, sin_ref,
                           decay_ref, xi_ref, zeta_ref, gl_ref, rnorm_ref, wout_ref, nffn_ref, wrhi_ref,
                           wrlo_ref, br_ref, h_ref, hn_ref, idx_ref, gate_ref, cnt_ref, nconv_ref, nret_ref,
                           ubuf, sbuf, mixbuf)

    @pl.when(step == N_TILES)
    def _():
        for dst, src in ((h_ref, hs_ref), (hn_ref, hns_ref), (idx_ref, idxs_ref), (gate_ref, gates_ref)):
            dst[0:DEC_BATCH, :] = src[...]
            dst[DEC_BATCH:TILE_L, :] = jnp.zeros((TILE_L - DEC_BATCH, dst.shape[1]), dst.dtype)
        cnt = cnt_ref[...]
        cnt_ref[...] = jnp.broadcast_to(jnp.sum(cnt, axis=0, keepdims=True), cnt.shape)


def _mixer_prompt_tile(i, x_ref, nmix_ref, win_ref, convw_ref, cos_ref, sin_ref, decay_ref, xi_ref,
                       zeta_ref, gl_ref, rnorm_ref, wout_ref, nffn_ref, wrhi_ref, wrlo_ref, br_ref,
                       h_ref, hn_ref, idx_ref, gate_ref, cnt_ref, nconv_ref, nret_ref,
                       ubuf, sbuf, mixbuf):
    @pl.when(i == 0)
    def _():
        ubuf[0:SUBLANES, :] = jnp.zeros((SUBLANES, CONV_W), F32)
        sbuf[...] = jnp.zeros_like(sbuf)

    x = x_ref[0]
    xn = _rmsnorm(x, nmix_ref[...]).astype(BF16)

    pc = _dot(xn, win_ref[:, 0:3 * CONV_W])
    u = pc[:, 2 * CONV_W:3 * CONV_W] * pc[:, 0:CONV_W]
    ubuf[SUBLANES:SUBLANES + TILE_L, :] = u
    conv = (convw_ref[0:1, :] * ubuf[SUBLANES - 2:SUBLANES - 2 + TILE_L, :]
            + convw_ref[1:2, :] * ubuf[SUBLANES - 1:SUBLANES - 1 + TILE_L, :]
            + convw_ref[2:3, :] * u)
    mixbuf[:, 0:CONV_W] = (pc[:, CONV_W:2 * CONV_W] * conv).astype(BF16)
    nconv_ref[0, 0] = u[TILE_L - (CONV_K - 1):TILE_L, :]
    ubuf[0:SUBLANES, :] = u[TILE_L - SUBLANES:TILE_L, :]

    pr = _dot(xn, win_ref[:, 3 * CONV_W:PROJ_W])
    cos2 = cos_ref[...]
    sin2 = sin_ref[...]
    for hh in range(RET_HEADS):
        lo = hh * RET_HD
        q = pr[:, lo:lo + RET_HD]
        k = pr[:, RET_W + lo:RET_W + lo + RET_HD]
        v = pr[:, 2 * RET_W + lo:2 * RET_W + lo + RET_HD]
        g = pr[:, 3 * RET_W + lo:3 * RET_W + lo + RET_HD]
        qr = _rotary(q, cos2, sin2)
        kr = _rotary(k, cos2, sin2) * (RET_HD ** -0.5)
        vb = v.astype(BF16)
        scores = _dot_nt(qr.astype(BF16), kr.astype(BF16)) * decay_ref[hh]
        state = sbuf[hh]
        o = _dot(scores.astype(BF16), vb) + _dot((qr * xi_ref[hh]).astype(BF16), state.astype(BF16))
        sbuf[hh] = gl_ref[hh] * state + _dot_tn((kr * zeta_ref[hh]).astype(BF16), vb)
        r = _group_norm_gate(o, g, rnorm_ref[:, lo:lo + RET_HD])
        mixbuf[:, CONV_W + lo:CONV_W + lo + RET_HD] = r.astype(BF16)
    nret_ref[0, 0] = sbuf[...]

    h = x + _dot(mixbuf[...], wout_ref[...])
    h_ref[...] = h
    hn = _rmsnorm(h, nffn_ref[...])
    hn_ref[...] = hn
    cnt_ref[...] += _route(hn, wrhi_ref[...], wrlo_ref[...], br_ref[...], gate_ref, idx_ref)


def _mixer_prompt(x, nmix, win, convw, cos2, sin2, decay, xi, zeta, gl, rnorm, wout, nffn, wrhi, wrlo, br,
                  h_s, hn_s, idx_s, gate_s, cnt_s):
    n_l = SEQ // TILE_L
    tile = lambda s: jnp.minimum(s, N_TILES - 1)
    row = lambda s: (s, 0)
    const2 = lambda s: (0, 0)
    const3 = lambda s: (0, 0, 0)
    return pl.pallas_call(
        _mixer_prompt_kernel,
        grid=(N_TILES + 1,),
        in_specs=[
            pl.BlockSpec((1, TILE_L, D_MODEL), lambda s: (tile(s) // n_l, tile(s) % n_l, 0)),
            pl.BlockSpec((1, D_MODEL), const2),
            pl.BlockSpec((D_MODEL, PROJ_W), const2),
            pl.BlockSpec((CONV_K, CONV_W), const2),
            pl.BlockSpec((TILE_L, RET_HD), lambda s: (tile(s) % n_l, 0)),
            pl.BlockSpec((TILE_L, RET_HD), lambda s: (tile(s) % n_l, 0)),
            pl.BlockSpec((RET_HEADS, TILE_L, TILE_L), const3),
            pl.BlockSpec((RET_HEADS, TILE_L, RET_HD), const3),
            pl.BlockSpec((RET_HEADS, TILE_L, RET_HD), const3),
            pl.BlockSpec((RET_HEADS, 1, RET_HD), const3),
            pl.BlockSpec((1, RET_W), const2),
            pl.BlockSpec((D_MODEL, D_MODEL), const2),
            pl.BlockSpec((1, D_MODEL), const2),
            pl.BlockSpec((D_MODEL, N_EXPERTS), const2),
            pl.BlockSpec((D_MODEL, N_EXPERTS), const2),
            pl.BlockSpec((1, N_EXPERTS), const2),
            pl.BlockSpec((DEC_BATCH, D_MODEL), const2),
            pl.BlockSpec((DEC_BATCH, D_MODEL), const2),
            pl.BlockSpec((DEC_BATCH, SUBLANES), const2),
            pl.BlockSpec((DEC_BATCH, SUBLANES), const2),
            pl.BlockSpec((SUBLANES, N_EXPERTS), const2),
        ],
        out_specs=[
            pl.BlockSpec((TILE_L, D_MODEL), row),
            pl.BlockSpec((TILE_L, D_MODEL), row),
            pl.BlockSpec((TILE_L, SUBLANES), row),
            pl.BlockSpec((TILE_L, SUBLANES), row),
            pl.BlockSpec((SUBLANES, N_EXPERTS), const2),
            pl.BlockSpec((1, 1, CONV_K - 1, CONV_W), lambda s: (0, tile(s) // n_l, 0, 0)),
            pl.BlockSpec((1, 1, RET_HEADS, RET_HD, RET_HD), lambda s: (0, tile(s) // n_l, 0, 0, 0)),
        ],
        out_shape=[
            jax.ShapeDtypeStruct((T_BUF, D_MODEL), F32),
            jax.ShapeDtypeStruct((T_BUF, D_MODEL), F32),
            jax.ShapeDtypeStruct((T_BUF, SUBLANES), I32),
            jax.ShapeDtypeStruct((T_BUF, SUBLANES), F32),
            jax.ShapeDtypeStruct((SUBLANES, N_EXPERTS), F32),
            jax.ShapeDtypeStruct((1, BATCH, CONV_K - 1, CONV_W), F32),
            jax.ShapeDtypeStruct((1, BATCH, RET_HEADS, RET_HD, RET_HD), F32),
        ],
        scratch_shapes=[
            pltpu.VMEM((TILE_L + SUBLANES, CONV_W), F32),
            pltpu.VMEM((RET_HEADS, RET_HD, RET_HD), F32),
            pltpu.VMEM((TILE_L, D_MODEL), BF16),
        ],
        compiler_params=pltpu.CompilerParams(
            dimension_semantics=("arbitrary",), vmem_limit_bytes=VMEM_LIMIT),
        name="mixer_prompt",
    )(x, nmix, win, convw, cos2, sin2, decay, xi, zeta, gl, rnorm, wout, nffn, wrhi, wrlo, br,
      h_s, hn_s, idx_s, gate_s, cnt_s)


SAMPLE_GROUP = 8
N_GROUPS = DEC_BATCH // SAMPLE_GROUP


def _mixer_sample_kernel(x_ref, sconv_ref, sret_ref, nmix_ref, win_ref, convw_ref, cos_ref, sin_ref,
                         gamma_ref, rnorm_ref, wout_ref, nffn_ref, wrhi_ref, wrlo_ref, br_ref,
                         h_ref, hn_ref, idx_ref, gate_ref, cnt_ref, nconv_ref, nret_ref,
                         qt_buf, kt_buf, v_buf, o_buf, g_buf, mixbuf):
    grp = pl.program_id(0)

    @pl.when(grp == 0)
    def _():
        x = x_ref[...]
        xn = _rmsnorm(x, nmix_ref[...]).astype(BF16)
        pc = _dot(xn, win_ref[:, 0:3 * CONV_W])
        u = pc[:, 2 * CONV_W:3 * CONV_W] * pc[:, 0:CONV_W]
        st = sconv_ref[...]
        conv = (convw_ref[0:1, :] * st[:, 0:CONV_W] + convw_ref[1:2, :] * st[:, CONV_W:2 * CONV_W]
                + convw_ref[2:3, :] * u)
        mixbuf[:, 0:CONV_W] = (pc[:, CONV_W:2 * CONV_W] * conv).astype(BF16)
        nconv_ref[:, 0:CONV_W] = st[:, CONV_W:2 * CONV_W]
        nconv_ref[:, CONV_W:2 * CONV_W] = u

        pr = _dot(xn, win_ref[:, 3 * CONV_W:PROJ_W])
        g_buf[...] = pr[:, 3 * RET_W:4 * RET_W]
        cos2 = cos_ref[...]
        sin2 = sin_ref[...]
        for hh in range(RET_HEADS):
            lo = hh * RET_HD
            qr = _rotary(pr[:, lo:lo + RET_HD], cos2, sin2)
            kr = _rotary(pr[:, RET_W + lo:RET_W + lo + RET_HD], cos2, sin2) * (RET_HD ** -0.5)
            v = pr[:, 2 * RET_W + lo:2 * RET_W + lo + RET_HD]
            v_buf[hh] = v
            o_buf[hh] = jnp.sum(qr * kr, axis=-1, keepdims=True) * v
            qt = qr.T
            kt = kr.T
            for gg in range(N_GROUPS):
                qt_buf[hh, gg] = qt[:, gg * SAMPLE_GROUP:(gg + 1) * SAMPLE_GROUP]
                kt_buf[hh, gg] = kt[:, gg * SAMPLE_GROUP:(gg + 1) * SAMPLE_GROUP]

    base = pl.multiple_of(grp * SAMPLE_GROUP, SAMPLE_GROUP)
    for hh in range(RET_HEADS):
        gam = gamma_ref[hh]
        for j in range(SAMPLE_GROUP):
            state = sret_ref[j, hh]
            qc = qt_buf[hh, grp, :, j:j + 1]
            kc = kt_buf[hh, grp, :, j:j + 1]
            vrow = v_buf[hh, pl.ds(base + j, 1), :]
            cross = jnp.sum((qc * gam) * state, axis=0, keepdims=True)
            o_buf[hh, pl.ds(base + j, 1), :] = o_buf[hh, pl.ds(base + j, 1), :] + cross
            nret_ref[j, hh] = gam * state + kc * vrow

    @pl.when(grp == N_GROUPS - 1)
    def _():
        for hh in range(RET_HEADS):
            lo = hh * RET_HD
            r = _group_norm_gate(o_buf[hh], g_buf[:, lo:lo + RET_HD], rnorm_ref[:, lo:lo + RET_HD])
            mixbuf[:, CONV_W + lo:CONV_W + lo + RET_HD] = r.astype(BF16)
        h = x_ref[...] + _dot(mixbuf[...], wout_ref[...])
        h_ref[...] = h
        hn = _rmsnorm(h, nffn_ref[...])
        hn_ref[...] = hn
        cnt_ref[...] = _route(hn, wrhi_ref[...], wrlo_ref[...], br_ref[...], gate_ref, idx_ref)


def _mixer_sample(x, sconv, sret, nmix, win, convw, cos2, sin2, gamma, rnorm, wout, nffn, wrhi, wrlo, br):
    const2 = lambda g: (0, 0)
    const3 = lambda g: (0, 0, 0)
    return pl.pallas_call(
        _mixer_sample_kernel,
        grid=(N_GROUPS,),
        in_specs=[
            pl.BlockSpec((DEC_BATCH, D_MODEL), const2),
            pl.BlockSpec((DEC_BATCH, (CONV_K - 1) * CONV_W), const2),
            pl.BlockSpec((SAMPLE_GROUP, RET_HEADS, RET_HD, RET_HD), lambda g: (g, 0, 0, 0)),
            pl.BlockSpec((1, D_MODEL), const2),
            pl.BlockSpec((D_MODEL, PROJ_W), const2),
            pl.BlockSpec((CONV_K, CONV_W), const2),
            pl.BlockSpec((1, RET_HD), const2),
            pl.BlockSpec((1, RET_HD), const2),
            pl.BlockSpec((RET_HEADS, 1, RET_HD), const3),
            pl.BlockSpec((1, RET_W), const2),
            pl.BlockSpec((D_MODEL, D_MODEL), const2),
            pl.BlockSpec((1, D_MODEL), const2),
            pl.BlockSpec((D_MODEL, N_EXPERTS), const2),
            pl.BlockSpec((D_MODEL, N_EXPERTS), const2),
            pl.BlockSpec((1, N_EXPERTS), const2),
        ],
        out_specs=[
            pl.BlockSpec((DEC_BATCH, D_MODEL), const2),
            pl.BlockSpec((DEC_BATCH, D_MODEL), const2),
            pl.BlockSpec((DEC_BATCH, SUBLANES), const2),
            pl.BlockSpec((DEC_BATCH, SUBLANES), const2),
            pl.BlockSpec((SUBLANES, N_EXPERTS), const2),
            pl.BlockSpec((DEC_BATCH, (CONV_K - 1) * CONV_W), const2),
            pl.BlockSpec((SAMPLE_GROUP, RET_HEADS, RET_HD, RET_HD), lambda g: (g, 0, 0, 0)),
        ],
        out_shape=[
            jax.ShapeDtypeStruct((DEC_BATCH, D_MODEL), F32),
            jax.ShapeDtypeStruct((DEC_BATCH, D_MODEL), F32),
            jax.ShapeDtypeStruct((DEC_BATCH, SUBLANES), I32),
            jax.ShapeDtypeStruct((DEC_BATCH, SUBLANES), F32),
            jax.ShapeDtypeStruct((SUBLANES, N_EXPERTS), F32),
            jax.ShapeDtypeStruct((DEC_BATCH, (CONV_K - 1) * CONV_W), F32),
            jax.ShapeDtypeStruct((DEC_BATCH, RET_HEADS, RET_HD, RET_HD), F32),
        ],
        scratch_shapes=[
            pltpu.VMEM((RET_HEADS, N_GROUPS, RET_HD, SAMPLE_GROUP), F32),
            pltpu.VMEM((RET_HEADS, N_GROUPS, RET_HD, SAMPLE_GROUP), F32),
            pltpu.VMEM((RET_HEADS, DEC_BATCH, RET_HD), F32),
            pltpu.VMEM((RET_HEADS, DEC_BATCH, RET_HD), F32),
            pltpu.VMEM((DEC_BATCH, RET_W), F32),
            pltpu.VMEM((DEC_BATCH, D_MODEL), BF16),
        ],
        compiler_params=pltpu.CompilerParams(
            dimension_semantics=("arbitrary",), vmem_limit_bytes=VMEM_LIMIT),
        name="mixer_sample",
    )(x, sconv, sret, nmix, win, convw, cos2, sin2, gamma, rnorm, wout, nffn, wrhi, wrlo, br)


def _route_rank_kernel(idx_ref, cnt_ref, dest_ref, be_ref, nb_ref, pstart_ref):
    cnt = cnt_ref[...]
    padded = ((cnt + (BLOCK_M - 1)) >> BLOCK_SHIFT) << BLOCK_SHIFT
    sub = lax.broadcasted_iota(I32, (N_EXPERTS, LANES), 0)
    pstart = jnp.zeros((N_EXPERTS, LANES), I32)
    for e in range(N_EXPERTS - 1):
        pstart = pstart + jnp.where(sub > e, padded[e:e + 1, :], 0)
    pend = pstart + padded
    pstart_ref[...] = pstart
    nb = pend[N_EXPERTS - 1:N_EXPERTS, :] >> BLOCK_SHIFT
    nb_ref[...] = nb

    blk = lax.broadcasted_iota(I32, (N_EXPERTS, BLOCK_LANES), 1)
    pend_w = jnp.concatenate([pend] * (BLOCK_LANES // LANES), axis=1)
    be = jnp.sum((pend_w <= blk * BLOCK_M).astype(I32), axis=0, keepdims=True)
    be = jnp.minimum(be, N_EXPERTS - 1)
    used = blk[0:1, :] < jnp.concatenate([nb] * (BLOCK_LANES // LANES), axis=1)
    last = jnp.max(jnp.where(used, be, 0), axis=1, keepdims=True)
    be_ref[...] = jnp.where(used, be, last)

    dest_ref[...] = jnp.zeros_like(dest_ref)
    upper = (lax.broadcasted_iota(I32, (LANES, LANES), 0)
             < lax.broadcasted_iota(I32, (LANES, LANES), 1)).astype(BF16)
    pstart_f = pstart.astype(F32)

    carry = jnp.zeros((N_EXPERTS, LANES), F32)
    for k in range(TOP_K):
        def chunk(c, carry, k=k):
            ids = idx_ref[k:k + 1, pl.ds(pl.multiple_of(c * LANES, LANES), LANES)]
            hit = sub == ids
            hit_f = hit.astype(F32)
            before = _dot(hit_f.astype(BF16), upper)
            pos = jnp.sum(jnp.where(hit, before + carry + pstart_f, 0.0), axis=0, keepdims=True)
            dest_ref[pl.ds(k * N_CHUNKS + c, 1), :] = pos.astype(I32)
            return carry + jnp.sum(hit_f, axis=1, keepdims=True)

        carry = lax.fori_loop(0, N_CHUNKS, chunk, carry)


def _route_rank(idx_lanes, cnt_col):
    return pl.pallas_call(
        _route_rank_kernel,
        out_shape=[
            jax.ShapeDtypeStruct((DEST_ROWS, LANES), I32),
            jax.ShapeDtypeStruct((1, BLOCK_LANES), I32),
            jax.ShapeDtypeStruct((1, LANES), I32),
            jax.ShapeDtypeStruct((N_EXPERTS, LANES), I32),
        ],
        name="route_rank",
    )(idx_lanes, cnt_col)


INVERT_ROWS = DEST_ROWS // 5


def _route_invert_kernel(cnt_ref, pstart_ref, dest_hbm, inv_ref, dest_s, sem):
    def pad_expert(e, _):
        n = cnt_ref[e]
        first = pstart_ref[e] + n
        n_pad = (((n + (BLOCK_M - 1)) >> BLOCK_SHIFT) << BLOCK_SHIFT) - n

        def pad_row(r, _):
            d = first + r
            inv_ref[d >> BLOCK_SHIFT, d & (BLOCK_M - 1)] = PAD_BASE + e * BLOCK_M + r
            return 0

        lax.fori_loop(0, n_pad, pad_row, 0)
        return 0

    part = pl.program_id(0)

    @pl.when(part == 0)
    def _():
        lax.fori_loop(0, N_EXPERTS, pad_expert, 0)

    row0 = part * INVERT_ROWS
    copy = pltpu.make_async_copy(dest_hbm.at[pl.ds(row0, INVERT_ROWS)], dest_s, sem)
    copy.start()
    copy.wait()

    def chunk(rr, _):
        row = row0 + rr
        k = row // N_CHUNKS
        tok0 = (row - k * N_CHUNKS) * LANES

        def one(lane, _):
            d = dest_s[rr, lane]
            inv_ref[d >> BLOCK_SHIFT, d & (BLOCK_M - 1)] = (tok0 + lane) * TOP_K + k
            return 0

        lax.fori_loop(0, LANES, one, 0, unroll=8)
        return 0

    lax.fori_loop(0, jnp.minimum(INVERT_ROWS, TOP_K * N_CHUNKS - row0), chunk, 0)


def _route_invert(dest, cnt, pstart):
    smem = pl.BlockSpec(memory_space=pltpu.SMEM)
    return pl.pallas_call(
        _route_invert_kernel,
        grid=(DEST_ROWS // INVERT_ROWS,),
        in_specs=[smem, smem, pl.BlockSpec(memory_space=pl.ANY)],
        out_specs=smem,
        out_shape=jax.ShapeDtypeStruct((N_BLOCKS, BLOCK_M), I32),
        scratch_shapes=[pltpu.SMEM((INVERT_ROWS, LANES), I32), pltpu.SemaphoreType.DMA(())],
        compiler_params=pltpu.CompilerParams(dimension_semantics=("arbitrary",)),
        name="route_invert",
    )(cnt, pstart, dest)


def _experts_kernel(inv_ref, be_ref, nb_ref, hn_hbm, wup_ref, bup_ref, wdn_ref, bdn_ref, y_hbm,
                    xbuf, obuf, wup_b, wdn_b, sem_in, sem_out):
    j = pl.program_id(0)
    nb = nb_ref[0]
    slot = j % 2

    def gather_rows(blk, s):
        def one(r, _):
            tok = jnp.minimum(inv_ref[blk, r] >> 2, T_ALL - 1)
            pltpu.make_async_copy(hn_hbm.at[pl.ds(tok, 1)], xbuf.at[s, pl.ds(r, 1)], sem_in.at[s]).start()
            return 0

        lax.fori_loop(0, BLOCK_M, one, 0, unroll=8)

    def scatter_rows(blk, s):
        def one(r, _):
            a = inv_ref[blk, r]
            row = (a & (TOP_K - 1)) * SLAB + (a >> 2)
            pltpu.make_async_copy(obuf.at[s, pl.ds(r, 1)], y_hbm.at[pl.ds(row, 1)], sem_out.at[s]).start()
            return 0

        lax.fori_loop(0, BLOCK_M, one, 0, unroll=8)

    def wait_gather(s):
        pltpu.make_async_copy(hn_hbm.at[pl.ds(0, BLOCK_M)], xbuf.at[s], sem_in.at[s]).wait()

    def wait_scatter(s):
        pltpu.make_async_copy(obuf.at[s], y_hbm.at[pl.ds(0, BLOCK_M)], sem_out.at[s]).wait()

    @pl.when(j == 0)
    def _():
        gather_rows(0, 0)

    @pl.when(j < nb)
    def _():
        @pl.when(j + 1 < nb)
        def _():
            gather_rows(j + 1, 1 - slot)

        wait_gather(slot)

        @pl.when(j >= 2)
        def _():
            wait_scatter(slot)

        @pl.when((j == 0) | (be_ref[j] != be_ref[jnp.maximum(j - 1, 0)]))
        def _():
            wup_b[...] = wup_ref[0].astype(BF16)
            wdn_b[...] = wdn_ref[0].astype(BF16)

        x = xbuf[slot].astype(BF16)
        hmid = _dot(x, wup_b[...]) + bup_ref[0]
        h_glu = jnp.minimum(hmid[:, 0:D_FF], SWIGLU_LIMIT)
        h_lin = jnp.clip(hmid[:, D_FF:2 * D_FF], -SWIGLU_LIMIT, SWIGLU_LIMIT)
        act = h_glu * jax.nn.sigmoid(SWIGLU_ALPHA * h_glu) * (h_lin + 1.0)
        obuf[slot] = _dot(act.astype(BF16), wdn_b[...]) + bdn_ref[0]
        scatter_rows(j, slot)

        @pl.when(j == nb - 1)
        def _():
            @pl.when(j >= 1)
            def _():
                wait_scatter(1 - slot)

            wait_scatter(slot)


def _experts(inv, be, nb, hn_all, w_up, b_up, w_down, b_down):
    grid_spec = pltpu.PrefetchScalarGridSpec(
        num_scalar_prefetch=3,
        grid=(N_BLOCKS,),
        in_specs=[
            pl.BlockSpec(memory_space=pl.ANY),
            pl.BlockSpec((1, D_MODEL, 2 * D_FF), lambda j, inv, be, nb: (be[j], 0, 0)),
            pl.BlockSpec((1, 1, 2 * D_FF), lambda j, inv, be, nb: (be[j], 0, 0)),
            pl.BlockSpec((1, D_FF, D_MODEL), lambda j, inv, be, nb: (be[j], 0, 0)),
            pl.BlockSpec((1, 1, D_MODEL), lambda j, inv, be, nb: (be[j], 0, 0)),
        ],
        out_specs=pl.BlockSpec(memory_space=pl.ANY),
        scratch_shapes=[
            pltpu.VMEM((2, BLOCK_M, D_MODEL), F32),
            pltpu.VMEM((2, BLOCK_M, D_MODEL), F32),
            pltpu.VMEM((D_MODEL, 2 * D_FF), BF16),
            pltpu.VMEM((D_FF, D_MODEL), BF16),
            pltpu.SemaphoreType.DMA((2,)),
            pltpu.SemaphoreType.DMA((2,)),
        ],
    )
    return pl.pallas_call(
        _experts_kernel,
        grid_spec=grid_spec,
        out_shape=jax.ShapeDtypeStruct((TOP_K * SLAB, D_MODEL), F32),
        compiler_params=pltpu.CompilerParams(
            dimension_semantics=("arbitrary",), vmem_limit_bytes=VMEM_LIMIT),
        name="experts",
    )(inv, be, nb, hn_all, w_up, b_up, w_down, b_down)


def _combine_kernel(ys_ref, gate_ref, h_ref, nfin_ref, out_ref):
    gates = gate_ref[...]
    acc = h_ref[...]
    for k in range(TOP_K):
        acc = acc + gates[:, k:k + 1] * ys_ref[k]
    out_ref[...] = _rmsnorm(acc, nfin_ref[...])


def _combine(ys, gates, h_all, nfin, rows, tile, first_block, name):
    return pl.pallas_call(
        _combine_kernel,
        grid=(rows // tile,),
        in_specs=[
            pl.BlockSpec((TOP_K, tile, D_MODEL), lambda i: (0, first_block + i, 0)),
            pl.BlockSpec((tile, SUBLANES), lambda i: (first_block + i, 0)),
            pl.BlockSpec((tile, D_MODEL), lambda i: (first_block + i, 0)),
            pl.BlockSpec((1, D_MODEL), lambda i: (0, 0)),
        ],
        out_specs=pl.BlockSpec((tile, D_MODEL), lambda i: (i, 0)),
        out_shape=jax.ShapeDtypeStruct((rows, D_MODEL), F32),
        compiler_params=pltpu.CompilerParams(
            dimension_semantics=("arbitrary",), vmem_limit_bytes=VMEM_LIMIT),
        name=name,
    )(ys, gates, h_all, nfin)


def _rope_tables(positions):
    half = RET_HD // 2
    inv = ROPE_BASE ** (-jnp.arange(half, dtype=F32) / half)
    ang = positions.astype(F32)[:, None] * inv[None, :]
    cos, sin = jnp.cos(ang), jnp.sin(ang)
    return jnp.concatenate([cos, cos], axis=-1), jnp.concatenate([-sin, sin], axis=-1)


def _decay_tables(log_gamma, c):
    pos = jnp.arange(c, dtype=F32)
    diff = pos[:, None] - pos[None, :]
    causal = diff >= 0
    lg = log_gamma[:, None, None]
    decay = jnp.where(causal, jnp.exp(lg * jnp.where(causal, diff, 0.0)), 0.0)
    xi = jnp.exp(log_gamma[:, None] * (pos[None, :] + 1.0))
    zeta = jnp.exp(log_gamma[:, None] * (c - 1.0 - pos[None, :]))
    wide = lambda t: jnp.broadcast_to(t[..., None], t.shape + (RET_HD,))
    gl = jnp.broadcast_to(jnp.exp(log_gamma * c)[:, None, None], (RET_HEADS, 1, RET_HD))
    return decay, wide(xi), wide(zeta), gl


def kernel(x_prompt, x_sample, state_conv, state_ret, norm_mix, w_in, conv_w, ret_norm, w_out, norm_ffn,
           w_router, b_router, w_up, b_up, w_down, b_down, norm_final):
    assert norm_mix.shape[0] == 1, "single trunk layer"
    nmix = norm_mix[0][None, :]
    nffn = norm_ffn[0][None, :]
    nfin = norm_final[None, :]
    rnorm = ret_norm[0][None, :]
    win = w_in[0].astype(BF16)
    wout = w_out[0].astype(BF16)
    wr = w_router[0]
    wrhi = wr.astype(BF16)
    wrlo = (wr - wrhi.astype(F32)).astype(BF16)
    br = b_router[0][None, :]
    convw = conv_w[0]

    log_gamma = jnp.log(1.0 - 2.0 ** (-5.0 - jnp.arange(RET_HEADS, dtype=F32)))
    decay, xi, zeta, gl = _decay_tables(log_gamma, TILE_L)
    cos_p, sin_p = _rope_tables(jnp.arange(SEQ, dtype=jnp.int32))
    cos_s, sin_s = _rope_tables(PAST_LEN + jnp.arange(1, dtype=jnp.int32))
    gamma1 = jnp.broadcast_to(jnp.exp(log_gamma)[:, None, None], (RET_HEADS, 1, RET_HD))

    h_s, hn_s, idx_s, gate_s, cnt_s, conv_s, ret_s = _mixer_sample(
        x_sample.reshape(DEC_BATCH, D_MODEL),
        state_conv[0].reshape(DEC_BATCH, (CONV_K - 1) * CONV_W),
        state_ret[0], nmix, win, convw, cos_s, sin_s, gamma1, rnorm, wout, nffn, wrhi, wrlo, br)

    h_all, hn_all, idx_all, gate_all, cnt, conv_p, ret_p = _mixer_prompt(
        x_prompt, nmix, win, convw, cos_p, sin_p, decay, xi, zeta, gl, rnorm, wout, nffn, wrhi, wrlo, br,
        h_s, hn_s, idx_s, gate_s, cnt_s)

    idx_lanes = idx_all[0:T_ALL, 0:TOP_K].T
    cnt_i = cnt[0].astype(I32)
    cnt_col = jnp.broadcast_to(cnt_i[:, None], (N_EXPERTS, LANES))
    dest, be, nb, pstart_col = _route_rank(idx_lanes, cnt_col)
    inv = _route_invert(dest, cnt_i, pstart_col[:, 0])

    ys = _experts(inv, be[0, 0:N_BLOCKS], nb[0, 0:1], hn_all,
                  w_up[0], b_up[0][:, None, :], w_down[0], b_down[0][:, None, :])
    ys = ys.reshape(TOP_K, SLAB, D_MODEL)

    y_prompt = _combine(ys, gate_all, h_all, nfin, T_PROMPT, FINAL_TILE, 0, "combine_prompt")
    y_sample = _combine(ys, gate_all, h_all, nfin, DEC_BATCH, DEC_BATCH, T_PROMPT // DEC_BATCH,
                        "combine_sample")

    return (y_prompt.reshape(BATCH, SEQ, D_MODEL),
            y_sample.reshape(DEC_BATCH, 1, D_MODEL),
            conv_p,
            ret_p,
            conv_s.reshape(1, DEC_BATCH, CONV_K - 1, CONV_W),
            ret_s.reshape(1, DEC_BATCH, RET_HEADS, RET_HD, RET_HD))
```

```python
import functools

import jax
import jax.numpy as jnp
from jax import lax
from jax.experimental import pallas as pl
from jax.experimental.pallas import tpu as pltpu

F32 = jnp.float32
BF16 = jnp.bfloat16
I32 = jnp.int32

D_MODEL = 1024
BATCH = 8
SEQ = 2048
DEC_BATCH = 128
PAST_LEN = 16384
CONV_W = 512
CONV_K = 3
RET_W = 512
RET_HEADS = 4
RET_HD = 128
ROPE_BASE = 10000.0
PROJ_W = 3 * CONV_W + 4 * RET_W
N_EXPERTS = 32
TOP_K = 4
D_FF = D_MODEL
SWIGLU_LIMIT = 7.0
SWIGLU_ALPHA = 1.702
RMS_EPS = 1e-6
GN_EPS = 1e-5

T_PROMPT = BATCH * SEQ
T_ALL = T_PROMPT + DEC_BATCH
N_ASSIGN = T_ALL * TOP_K

LANES = 128
SUBLANES = 8
TOKEN_ROWS = D_MODEL // LANES
TILE_L = 256
N_TILES = T_PROMPT // TILE_L
T_BUF = (N_TILES + 1) * TILE_L
BLOCK_M = 256
BLOCK_SHIFT = 8
N_BLOCKS = N_ASSIGN // BLOCK_M + N_EXPERTS
BLOCK_LANES = 3 * LANES
DUMP_ROWS = (N_EXPERTS + 1) * BLOCK_M // TOP_K
SLAB = T_ALL + DUMP_ROWS
PAD_BASE = TOP_K * T_ALL
N_CHUNKS = T_ALL // LANES
DEST_ROWS = 520
FINAL_TILE = 512
VMEM_LIMIT = 56 * 1024 * 1024

assert N_ASSIGN % BLOCK_M == 0 and N_BLOCKS <= BLOCK_LANES and (1 << BLOCK_SHIFT) == BLOCK_M
assert T_ALL % LANES == 0 and TOP_K * N_CHUNKS <= DEST_ROWS


def _rmsnorm(x, g):
    ms = jnp.mean(x * x, axis=-1, keepdims=True)
    return x * lax.rsqrt(ms + RMS_EPS) * g


def _store_token_tiles(ref, val):
    rows = val.shape[0]
    for s in range(TOKEN_ROWS):
        ref[pl.ds(s, rows, stride=TOKEN_ROWS), :] = val[:, s * LANES:(s + 1) * LANES]


def _load_token_tiles(ref):
    rows = ref.shape[0] // TOKEN_ROWS
    return jnp.concatenate([ref[pl.ds(s, rows, stride=TOKEN_ROWS), :] for s in range(TOKEN_ROWS)], axis=1)


def _dot(a, b):
    return jnp.dot(a, b, preferred_element_type=F32)


def _dot_nt(a, b):
    return lax.dot_general(a, b, (((1,), (1,)), ((), ())), preferred_element_type=F32)


def _dot_tn(a, b):
    return lax.dot_general(a, b, (((0,), (0,)), ((), ())), preferred_element_type=F32)


def _rotary(x, cos2, sin2):
    return x * cos2 + pltpu.roll(x, RET_HD // 2, 1) * sin2


def _group_norm_gate(o, g, w):
    mu = jnp.mean(o, axis=-1, keepdims=True)
    d = o - mu
    var = jnp.mean(d * d, axis=-1, keepdims=True)
    return (g * jax.nn.sigmoid(g)) * (d * lax.rsqrt(var + GN_EPS) * w)


def _route(hn, wr_hi, wr_lo, b_router, gate_ref, idx_ref):
    rows = hn.shape[0]
    hi = hn.astype(BF16)
    lo = (hn - hi.astype(F32)).astype(BF16)
    logits = _dot(hi, wr_hi) + _dot(lo, wr_hi) + _dot(hi, wr_lo) + b_router
    lane = lax.broadcasted_iota(I32, (rows, N_EXPERTS), 1)
    vals, sels = [], []
    onehot = jnp.zeros((rows, N_EXPERTS), F32)
    for _ in range(TOP_K):
        m = jnp.max(logits, axis=-1, keepdims=True)
        sel = jnp.min(jnp.where(logits == m, lane, N_EXPERTS), axis=-1, keepdims=True)
        hit = lane == sel
        onehot = onehot + hit.astype(F32)
        logits = jnp.where(hit, -jnp.inf, logits)
        vals.append(m)
        sels.append(sel)
    ex = [jnp.exp(v - vals[0]) for v in vals]
    den = ex[0] + ex[1] + ex[2] + ex[3]
    zero = jnp.zeros((rows, SUBLANES - TOP_K), F32)
    gate_ref[...] = jnp.concatenate([e / den for e in ex] + [zero], axis=1)
    idx_ref[...] = jnp.concatenate(sels + [zero.astype(I32)], axis=1)
    return jnp.sum(onehot.reshape(rows // SUBLANES, SUBLANES, N_EXPERTS), axis=0)


def _mixer_prompt_kernel(x_ref, nmix_ref, win_ref, convw_ref, cos_ref, sin_ref, decay_ref, xi_ref,
                         zeta_ref, gl_ref, rnorm_ref, wout_ref, nffn_ref, wrhi_ref, wrlo_ref, br_ref,
                         hs_ref, hns_ref, idxs_ref, gates_ref, cnts_ref,
                         h_ref, hn_ref, idx_ref, gate_ref, cnt_ref, nconv_ref, nret_ref,
                         ubuf, sbuf, mixbuf):
    step = pl.program_id(0)

    @pl.when(step == 0)
    def _():
        cnt_ref[...] = cnts_ref[...]

    @pl.when(step < N_TILES)
    def _():
        _mixer_prompt_tile(step % (SEQ // TILE_L), x_ref, nmix_ref, win_ref, convw_ref, cos_ref, sin_ref,
                           decay_ref, xi_ref, zeta_ref, gl_ref, rnorm_ref, wout_ref, nffn_ref, wrhi_ref,
                           wrlo_ref, br_ref, h_ref, hn_ref, idx_ref, gate_ref, cnt_ref, nconv_ref, nret_ref,
                           ubuf, sbuf, mixbuf)

    @pl.when(step == N_TILES)
    def _():
        for dst, src in ((h_ref, hs_ref), (hn_ref, hns_ref), (idx_ref, idxs_ref), (gate_ref, gates_ref)):
            n_src, n_dst = src.shape[0], dst.shape[0]
            dst[0:n_src, :] = src[...]
            dst[n_src:n_dst, :] = jnp.zeros((n_dst - n_src, dst.shape[1]), dst.dtype)
        cnt = cnt_ref[...]
        cnt_ref[...] = jnp.broadcast_to(jnp.sum(cnt, axis=0, keepdims=True), cnt.shape)


def _mixer_prompt_tile(i, x_ref, nmix_ref, win_ref, convw_ref, cos_ref, sin_ref, decay_ref, xi_ref,
                       zeta_ref, gl_ref, rnorm_ref, wout_ref, nffn_ref, wrhi_ref, wrlo_ref, br_ref,
                       h_ref, hn_ref, idx_ref, gate_ref, cnt_ref, nconv_ref, nret_ref,
                       ubuf, sbuf, mixbuf):
    @pl.when(i == 0)
    def _():
        ubuf[0:SUBLANES, :] = jnp.zeros((SUBLANES, CONV_W), F32)
        sbuf[...] = jnp.zeros_like(sbuf)

    x = x_ref[0]
    xn = _rmsnorm(x, nmix_ref[...]).astype(BF16)

    pc = _dot(xn, win_ref[:, 0:3 * CONV_W])
    u = pc[:, 2 * CONV_W:3 * CONV_W] * pc[:, 0:CONV_W]
    ubuf[SUBLANES:SUBLANES + TILE_L, :] = u
    conv = (convw_ref[0:1, :] * ubuf[SUBLANES - 2:SUBLANES - 2 + TILE_L, :]
            + convw_ref[1:2, :] * ubuf[SUBLANES - 1:SUBLANES - 1 + TILE_L, :]
            + convw_ref[2:3, :] * u)
    mixbuf[:, 0:CONV_W] = (pc[:, CONV_W:2 * CONV_W] * conv).astype(BF16)
    nconv_ref[0, 0] = u[TILE_L - (CONV_K - 1):TILE_L, :]
    ubuf[0:SUBLANES, :] = u[TILE_L - SUBLANES:TILE_L, :]

    pr = _dot(xn, win_ref[:, 3 * CONV_W:PROJ_W])
    cos2 = cos_ref[...]
    sin2 = sin_ref[...]
    for hh in range(RET_HEADS):
        lo = hh * RET_HD
        q = pr[:, lo:lo + RET_HD]
        k = pr[:, RET_W + lo:RET_W + lo + RET_HD]
        v = pr[:, 2 * RET_W + lo:2 * RET_W + lo + RET_HD]
        g = pr[:, 3 * RET_W + lo:3 * RET_W + lo + RET_HD]
        qr = _rotary(q, cos2, sin2)
        kr = _rotary(k, cos2, sin2) * (RET_HD ** -0.5)
        vb = v.astype(BF16)
        scores = _dot_nt(qr.astype(BF16), kr.astype(BF16)) * decay_ref[hh]
        state = sbuf[hh]
        o = _dot(scores.astype(BF16), vb) + _dot((qr * xi_ref[hh]).astype(BF16), state.astype(BF16))
        sbuf[hh] = gl_ref[hh] * state + _dot_tn((kr * zeta_ref[hh]).astype(BF16), vb)
        r = _group_norm_gate(o, g, rnorm_ref[:, lo:lo + RET_HD])
        mixbuf[:, CONV_W + lo:CONV_W + lo + RET_HD] = r.astype(BF16)
    nret_ref[0, 0] = sbuf[...]

    h = x + _dot(mixbuf[...], wout_ref[...])
    h_ref[...] = h
    hn = _rmsnorm(h, nffn_ref[...])
    _store_token_tiles(hn_ref, hn)
    cnt_ref[...] += _route(hn, wrhi_ref[...], wrlo_ref[...], br_ref[...], gate_ref, idx_ref)


def _mixer_prompt(x, nmix, win, convw, cos2, sin2, decay, xi, zeta, gl, rnorm, wout, nffn, wrhi, wrlo, br,
                  h_s, hn_s, idx_s, gate_s, cnt_s):
    n_l = SEQ // TILE_L
    tile = lambda s: jnp.minimum(s, N_TILES - 1)
    row = lambda s: (s, 0)
    const2 = lambda s: (0, 0)
    const3 = lambda s: (0, 0, 0)
    return pl.pallas_call(
        _mixer_prompt_kernel,
        grid=(N_TILES + 1,),
        in_specs=[
            pl.BlockSpec((1, TILE_L, D_MODEL), lambda s: (tile(s) // n_l, tile(s) % n_l, 0)),
            pl.BlockSpec((1, D_MODEL), const2),
            pl.BlockSpec((D_MODEL, PROJ_W), const2),
            pl.BlockSpec((CONV_K, CONV_W), const2),
            pl.BlockSpec((TILE_L, RET_HD), lambda s: (tile(s) % n_l, 0)),
            pl.BlockSpec((TILE_L, RET_HD), lambda s: (tile(s) % n_l, 0)),
            pl.BlockSpec((RET_HEADS, TILE_L, TILE_L), const3),
            pl.BlockSpec((RET_HEADS, TILE_L, RET_HD), const3),
            pl.BlockSpec((RET_HEADS, TILE_L, RET_HD), const3),
            pl.BlockSpec((RET_HEADS, 1, RET_HD), const3),
            pl.BlockSpec((1, RET_W), const2),
            pl.BlockSpec((D_MODEL, D_MODEL), const2),
            pl.BlockSpec((1, D_MODEL), const2),
            pl.BlockSpec((D_MODEL, N_EXPERTS), const2),
            pl.BlockSpec((D_MODEL, N_EXPERTS), const2),
            pl.BlockSpec((1, N_EXPERTS), const2),
            pl.BlockSpec((DEC_BATCH, D_MODEL), const2),
            pl.BlockSpec((DEC_BATCH * TOKEN_ROWS, LANES), const2),
            pl.BlockSpec((DEC_BATCH, SUBLANES), const2),
            pl.BlockSpec((DEC_BATCH, SUBLANES), const2),
            pl.BlockSpec((SUBLANES, N_EXPERTS), const2),
        ],
        out_specs=[
            pl.BlockSpec((TILE_L, D_MODEL), row),
            pl.BlockSpec((TILE_L * TOKEN_ROWS, LANES), row),
            pl.BlockSpec((TILE_L, SUBLANES), row),
            pl.BlockSpec((TILE_L, SUBLANES), row),
            pl.BlockSpec((SUBLANES, N_EXPERTS), const2),
            pl.BlockSpec((1, 1, CONV_K - 1, CONV_W), lambda s: (0, tile(s) // n_l, 0, 0)),
            pl.BlockSpec((1, 1, RET_HEADS, RET_HD, RET_HD), lambda s: (0, tile(s) // n_l, 0, 0, 0)),
        ],
        out_shape=[
            jax.ShapeDtypeStruct((T_BUF, D_MODEL), F32),
            jax.ShapeDtypeStruct((T_BUF * TOKEN_ROWS, LANES), F32),
            jax.ShapeDtypeStruct((T_BUF, SUBLANES), I32),
            jax.ShapeDtypeStruct((T_BUF, SUBLANES), F32),
            jax.ShapeDtypeStruct((SUBLANES, N_EXPERTS), F32),
            jax.ShapeDtypeStruct((1, BATCH, CONV_K - 1, CONV_W), F32),
            jax.ShapeDtypeStruct((1, BATCH, RET_HEADS, RET_HD, RET_HD), F32),
        ],
        scratch_shapes=[
            pltpu.VMEM((TILE_L + SUBLANES, CONV_W), F32),
            pltpu.VMEM((RET_HEADS, RET_HD, RET_HD), F32),
            pltpu.VMEM((TILE_L, D_MODEL), BF16),
        ],
        compiler_params=pltpu.CompilerParams(
            dimension_semantics=("arbitrary",), vmem_limit_bytes=VMEM_LIMIT),
        name="mixer_prompt",
    )(x, nmix, win, convw, cos2, sin2, decay, xi, zeta, gl, rnorm, wout, nffn, wrhi, wrlo, br,
      h_s, hn_s, idx_s, gate_s, cnt_s)


SAMPLE_GROUP = 8
N_GROUPS = DEC_BATCH // SAMPLE_GROUP


def _mixer_sample_kernel(x_ref, sconv_ref, sret_ref, nmix_ref, win_ref, convw_ref, cos_ref, sin_ref,
                         gamma_ref, rnorm_ref, wout_ref, nffn_ref, wrhi_ref, wrlo_ref, br_ref,
                         h_ref, hn_ref, idx_ref, gate_ref, cnt_ref, nconv_ref, nret_ref,
                         qt_buf, kt_buf, v_buf, o_buf, g_buf, mixbuf):
    grp = pl.program_id(0)

    @pl.when(grp == 0)
    def _():
        x = x_ref[...]
        xn = _rmsnorm(x, nmix_ref[...]).astype(BF16)
        pc = _dot(xn, win_ref[:, 0:3 * CONV_W])
        u = pc[:, 2 * CONV_W:3 * CONV_W] * pc[:, 0:CONV_W]
        st = sconv_ref[...]
        conv = (convw_ref[0:1, :] * st[:, 0:CONV_W] + convw_ref[1:2, :] * st[:, CONV_W:2 * CONV_W]
                + convw_ref[2:3, :] * u)
        mixbuf[:, 0:CONV_W] = (pc[:, CONV_W:2 * CONV_W] * conv).astype(BF16)
        nconv_ref[:, 0:CONV_W] = st[:, CONV_W:2 * CONV_W]
        nconv_ref[:, CONV_W:2 * CONV_W] = u

        pr = _dot(xn, win_ref[:, 3 * CONV_W:PROJ_W])
        g_buf[...] = pr[:, 3 * RET_W:4 * RET_W]
        cos2 = cos_ref[...]
        sin2 = sin_ref[...]
        for hh in range(RET_HEADS):
            lo = hh * RET_HD
            qr = _rotary(pr[:, lo:lo + RET_HD], cos2, sin2)
            kr = _rotary(pr[:, RET_W + lo:RET_W + lo + RET_HD], cos2, sin2) * (RET_HD ** -0.5)
            v = pr[:, 2 * RET_W + lo:2 * RET_W + lo + RET_HD]
            v_buf[hh] = v
            o_buf[hh] = jnp.sum(qr * kr, axis=-1, keepdims=True) * v
            qt = qr.T
            kt = kr.T
            for gg in range(N_GROUPS):
                qt_buf[hh, gg] = qt[:, gg * SAMPLE_GROUP:(gg + 1) * SAMPLE_GROUP]
                kt_buf[hh, gg] = kt[:, gg * SAMPLE_GROUP:(gg + 1) * SAMPLE_GROUP]

    base = pl.multiple_of(grp * SAMPLE_GROUP, SAMPLE_GROUP)
    for hh in range(RET_HEADS):
        gam = gamma_ref[hh]
        for j in range(SAMPLE_GROUP):
            state = sret_ref[j, hh]
            qc = qt_buf[hh, grp, :, j:j + 1]
            kc = kt_buf[hh, grp, :, j:j + 1]
            vrow = v_buf[hh, pl.ds(base + j, 1), :]
            cross = jnp.sum((qc * gam) * state, axis=0, keepdims=True)
            o_buf[hh, pl.ds(base + j, 1), :] = o_buf[hh, pl.ds(base + j, 1), :] + cross
            nret_ref[j, hh] = gam * state + kc * vrow

    @pl.when(grp == N_GROUPS - 1)
    def _():
        for hh in range(RET_HEADS):
            lo = hh * RET_HD
            r = _group_norm_gate(o_buf[hh], g_buf[:, lo:lo + RET_HD], rnorm_ref[:, lo:lo + RET_HD])
            mixbuf[:, CONV_W + lo:CONV_W + lo + RET_HD] = r.astype(BF16)
        h = x_ref[...] + _dot(mixbuf[...], wout_ref[...])
        h_ref[...] = h
        hn = _rmsnorm(h, nffn_ref[...])
        _store_token_tiles(hn_ref, hn)
        cnt_ref[...] = _route(hn, wrhi_ref[...], wrlo_ref[...], br_ref[...], gate_ref, idx_ref)


def _mixer_sample(x, sconv, sret, nmix, win, convw, cos2, sin2, gamma, rnorm, wout, nffn, wrhi, wrlo, br):
    const2 = lambda g: (0, 0)
    const3 = lambda g: (0, 0, 0)
    return pl.pallas_call(
        _mixer_sample_kernel,
        grid=(N_GROUPS,),
        in_specs=[
            pl.BlockSpec((DEC_BATCH, D_MODEL), const2),
            pl.BlockSpec((DEC_BATCH, (CONV_K - 1) * CONV_W), const2),
            pl.BlockSpec((SAMPLE_GROUP, RET_HEADS, RET_HD, RET_HD), lambda g: (g, 0, 0, 0)),
            pl.BlockSpec((1, D_MODEL), const2),
            pl.BlockSpec((D_MODEL, PROJ_W), const2),
            pl.BlockSpec((CONV_K, CONV_W), const2),
            pl.BlockSpec((1, RET_HD), const2),
            pl.BlockSpec((1, RET_HD), const2),
            pl.BlockSpec((RET_HEADS, 1, RET_HD), const3),
            pl.BlockSpec((1, RET_W), const2),
            pl.BlockSpec((D_MODEL, D_MODEL), const2),
            pl.BlockSpec((1, D_MODEL), const2),
            pl.BlockSpec((D_MODEL, N_EXPERTS), const2),
            pl.BlockSpec((D_MODEL, N_EXPERTS), const2),
            pl.BlockSpec((1, N_EXPERTS), const2),
        ],
        out_specs=[
            pl.BlockSpec((DEC_BATCH, D_MODEL), const2),
            pl.BlockSpec((DEC_BATCH * TOKEN_ROWS, LANES), const2),
            pl.BlockSpec((DEC_BATCH, SUBLANES), const2),
            pl.BlockSpec((DEC_BATCH, SUBLANES), const2),
            pl.BlockSpec((SUBLANES, N_EXPERTS), const2),
            pl.BlockSpec((DEC_BATCH, (CONV_K - 1) * CONV_W), const2),
            pl.BlockSpec((SAMPLE_GROUP, RET_HEADS, RET_HD, RET_HD), lambda g: (g, 0, 0, 0)),
        ],
        out_shape=[
            jax.ShapeDtypeStruct((DEC_BATCH, D_MODEL), F32),
            jax.ShapeDtypeStruct((DEC_BATCH * TOKEN_ROWS, LANES), F32),
            jax.ShapeDtypeStruct((DEC_BATCH, SUBLANES), I32),
            jax.ShapeDtypeStruct((DEC_BATCH, SUBLANES), F32),
            jax.ShapeDtypeStruct((SUBLANES, N_EXPERTS), F32),
            jax.ShapeDtypeStruct((DEC_BATCH, (CONV_K - 1) * CONV_W), F32),
            jax.ShapeDtypeStruct((DEC_BATCH, RET_HEADS, RET_HD, RET_HD), F32),
        ],
        scratch_shapes=[
            pltpu.VMEM((RET_HEADS, N_GROUPS, RET_HD, SAMPLE_GROUP), F32),
            pltpu.VMEM((RET_HEADS, N_GROUPS, RET_HD, SAMPLE_GROUP), F32),
            pltpu.VMEM((RET_HEADS, DEC_BATCH, RET_HD), F32),
            pltpu.VMEM((RET_HEADS, DEC_BATCH, RET_HD), F32),
            pltpu.VMEM((DEC_BATCH, RET_W), F32),
            pltpu.VMEM((DEC_BATCH, D_MODEL), BF16),
        ],
        compiler_params=pltpu.CompilerParams(
            dimension_semantics=("arbitrary",), vmem_limit_bytes=VMEM_LIMIT),
        name="mixer_sample",
    )(x, sconv, sret, nmix, win, convw, cos2, sin2, gamma, rnorm, wout, nffn, wrhi, wrlo, br)


def _route_rank_kernel(idx_ref, cnt_ref, dest_ref, be_ref, nb_ref, pstart_ref):
    cnt = cnt_ref[...]
    padded = ((cnt + (BLOCK_M - 1)) >> BLOCK_SHIFT) << BLOCK_SHIFT
    sub = lax.broadcasted_iota(I32, (N_EXPERTS, LANES), 0)
    pstart = jnp.zeros((N_EXPERTS, LANES), I32)
    for e in range(N_EXPERTS - 1):
        pstart = pstart + jnp.where(sub > e, padded[e:e + 1, :], 0)
    pend = pstart + padded
    pstart_ref[...] = pstart
    nb = pend[N_EXPERTS - 1:N_EXPERTS, :] >> BLOCK_SHIFT
    nb_ref[...] = nb

    blk = lax.broadcasted_iota(I32, (N_EXPERTS, BLOCK_LANES), 1)
    pend_w = jnp.concatenate([pend] * (BLOCK_LANES // LANES), axis=1)
    be = jnp.sum((pend_w <= blk * BLOCK_M).astype(I32), axis=0, keepdims=True)
    be = jnp.minimum(be, N_EXPERTS - 1)
    used = blk[0:1, :] < jnp.concatenate([nb] * (BLOCK_LANES // LANES), axis=1)
    last = jnp.max(jnp.where(used, be, 0), axis=1, keepdims=True)
    be_ref[...] = jnp.where(used, be, last)

    dest_ref[...] = jnp.zeros_like(dest_ref)
    upper = (lax.broadcasted_iota(I32, (LANES, LANES), 0)
             < lax.broadcasted_iota(I32, (LANES, LANES), 1)).astype(BF16)
    pstart_f = pstart.astype(F32)

    carry = jnp.zeros((N_EXPERTS, LANES), F32)
    for k in range(TOP_K):
        def chunk(c, carry, k=k):
            ids = idx_ref[k:k + 1, pl.ds(pl.multiple_of(c * LANES, LANES), LANES)]
            hit = sub == ids
            hit_f = hit.astype(F32)
            before = _dot(hit_f.astype(BF16), upper)
            pos = jnp.sum(jnp.where(hit, before + carry + pstart_f, 0.0), axis=0, keepdims=True)
            dest_ref[pl.ds(k * N_CHUNKS + c, 1), :] = pos.astype(I32)
            return carry + jnp.sum(hit_f, axis=1, keepdims=True)

        carry = lax.fori_loop(0, N_CHUNKS, chunk, carry)


def _route_rank(idx_lanes, cnt_col):
    return pl.pallas_call(
        _route_rank_kernel,
        out_shape=[
            jax.ShapeDtypeStruct((DEST_ROWS, LANES), I32),
            jax.ShapeDtypeStruct((1, BLOCK_LANES), I32),
            jax.ShapeDtypeStruct((1, LANES), I32),
            jax.ShapeDtypeStruct((N_EXPERTS, LANES), I32),
        ],
        name="route_rank",
    )(idx_lanes, cnt_col)


INVERT_ROWS = DEST_ROWS // 5
INVERT_GROUP = 16


def _route_invert_kernel(cnt_ref, pstart_ref, dest_hbm, inv_ref, dest_s, sem):
    def pad_expert(e, _):
        n = cnt_ref[e]
        first = pstart_ref[e] + n
        n_pad = (((n + (BLOCK_M - 1)) >> BLOCK_SHIFT) << BLOCK_SHIFT) - n

        def pad_row(r, _):
            inv_ref[BLOCK_M + first + r] = PAD_BASE + e * BLOCK_M + r
            return 0

        lax.fori_loop(0, n_pad, pad_row, 0)
        return 0

    part = pl.program_id(0)

    @pl.when(part == 0)
    def _():
        lax.fori_loop(0, N_EXPERTS, pad_expert, 0)

        def lead_row(r, _):
            inv_ref[r] = PAD_BASE + N_EXPERTS * BLOCK_M + r
            return 0

        lax.fori_loop(0, BLOCK_M, lead_row, 0)

    row0 = part * INVERT_ROWS
    copy = pltpu.make_async_copy(
        dest_hbm.at[pl.ds(pl.multiple_of(row0 * LANES, LANES), INVERT_ROWS * LANES)], dest_s, sem)
    copy.start()
    copy.wait()

    def chunk(rr, _):
        row = row0 + rr
        k = row // N_CHUNKS
        val0 = (row - k * N_CHUNKS) * (LANES * TOP_K) + k
        base = rr * LANES
        for g in range(0, LANES, INVERT_GROUP):
            dests = [dest_s[base + g + i] for i in range(INVERT_GROUP)]
            for i, d in enumerate(dests):
                inv_ref[BLOCK_M + d] = val0 + (g + i) * TOP_K
        return 0

    lax.fori_loop(0, jnp.minimum(INVERT_ROWS, TOP_K * N_CHUNKS - row0), chunk, 0)


def _route_invert(dest, cnt, pstart):
    smem = pl.BlockSpec(memory_space=pltpu.SMEM)
    return pl.pallas_call(
        _route_invert_kernel,
        grid=(DEST_ROWS // INVERT_ROWS,),
        in_specs=[smem, smem, pl.BlockSpec(memory_space=pl.ANY)],
        out_specs=smem,
        out_shape=jax.ShapeDtypeStruct(((N_BLOCKS + 1) * BLOCK_M,), I32),
        scratch_shapes=[pltpu.SMEM((INVERT_ROWS * LANES,), I32), pltpu.SemaphoreType.DMA(())],
        compiler_params=pltpu.CompilerParams(dimension_semantics=("arbitrary",)),
        name="route_invert",
    )(cnt, pstart, dest)


def _experts_kernel(inv_ref, be_ref, nb_ref, hn_hbm, wup_ref, bup_ref, wdn_ref, bdn_ref, y_hbm,
                    xbuf0, xbuf1, obuf0, obuf1, wup_b, wdn_b, sem_in, sem_out):
    j = pl.program_id(0)
    nb = nb_ref[0]
    xbufs, obufs = (xbuf0, xbuf1), (obuf0, obuf1)

    def token_rows(t):
        if isinstance(t, int):
            return pl.ds(t * TOKEN_ROWS, TOKEN_ROWS)
        return pl.ds(pl.multiple_of(t * TOKEN_ROWS, TOKEN_ROWS), TOKEN_ROWS)

    def gather_rows(blk, s):
        base = (blk + 1) * BLOCK_M
        for r in range(BLOCK_M):
            tok = jnp.minimum(inv_ref[base + r] >> 2, T_ALL - 1)
            pltpu.make_async_copy(hn_hbm.at[token_rows(tok)], xbufs[s].at[token_rows(r)], sem_in.at[s]).start()

    def scatter_rows(blk, s):
        base = (blk + 1) * BLOCK_M
        for r in range(BLOCK_M):
            a = inv_ref[base + r]
            row = (a & (TOP_K - 1)) * SLAB + (a >> 2)
            pltpu.make_async_copy(obufs[s].at[token_rows(r)], y_hbm.at[token_rows(row)], sem_out.at[s]).start()

    def wait_gather(s):
        pltpu.make_async_copy(hn_hbm.at[pl.ds(0, BLOCK_M * TOKEN_ROWS)], xbufs[s], sem_in.at[s]).wait()

    def wait_scatter(s):
        pltpu.make_async_copy(obufs[s], y_hbm.at[pl.ds(0, BLOCK_M * TOKEN_ROWS)], sem_out.at[s]).wait()

    def mlp(s):
        x = _load_token_tiles(xbufs[s]).astype(BF16)
        hmid = _dot(x, wup_b[...]) + bup_ref[0]
        h_glu = jnp.minimum(hmid[:, 0:D_FF], SWIGLU_LIMIT)
        h_lin = jnp.clip(hmid[:, D_FF:2 * D_FF], -SWIGLU_LIMIT, SWIGLU_LIMIT)
        act = h_glu * jax.nn.sigmoid(SWIGLU_ALPHA * h_glu) * (h_lin + 1.0)
        _store_token_tiles(obufs[s], _dot(act.astype(BF16), wdn_b[...]) + bdn_ref[0])

    @pl.when(j == 0)
    def _():
        gather_rows(0, 0)
        obuf1[...] = jnp.zeros_like(obuf1)

    @pl.when((j < nb) & ((j == 0) | (be_ref[j] != be_ref[jnp.maximum(j - 1, 0)])))
    def _():
        wup_b[...] = wup_ref[0].astype(BF16)
        wdn_b[...] = wdn_ref[0].astype(BF16)

    for s in range(2):
        @pl.when((j < nb) & (j % 2 == s))
        def _(s=s):
            wait_gather(s)

            @pl.when(j >= 1)
            def _():
                wait_scatter(s)

            gather_rows(jnp.minimum(j + 1, nb - 1), 1 - s)
            scatter_rows(j - 1, 1 - s)
            mlp(s)

            @pl.when(j == nb - 1)
            def _():
                scatter_rows(j, s)
                wait_scatter(1 - s)
                wait_scatter(s)
                wait_gather(1 - s)


def _experts(inv, be, nb, hn_all, w_up, b_up, w_down, b_down):
    grid_spec = pltpu.PrefetchScalarGridSpec(
        num_scalar_prefetch=3,
        grid=(N_BLOCKS,),
        in_specs=[
            pl.BlockSpec(memory_space=pl.ANY),
            pl.BlockSpec((1, D_MODEL, 2 * D_FF), lambda j, inv, be, nb: (be[j], 0, 0)),
            pl.BlockSpec((1, 1, 2 * D_FF), lambda j, inv, be, nb: (be[j], 0, 0)),
            pl.BlockSpec((1, D_FF, D_MODEL), lambda j, inv, be, nb: (be[j], 0, 0)),
            pl.BlockSpec((1, 1, D_MODEL), lambda j, inv, be, nb: (be[j], 0, 0)),
        ],
        out_specs=pl.BlockSpec(memory_space=pl.ANY),
        scratch_shapes=[
            pltpu.VMEM((BLOCK_M * TOKEN_ROWS, LANES), F32),
            pltpu.VMEM((BLOCK_M * TOKEN_ROWS, LANES), F32),
            pltpu.VMEM((BLOCK_M * TOKEN_ROWS, LANES), F32),
            pltpu.VMEM((BLOCK_M * TOKEN_ROWS, LANES), F32),
            pltpu.VMEM((D_MODEL, 2 * D_FF), BF16),
            pltpu.VMEM((D_FF, D_MODEL), BF16),
            pltpu.SemaphoreType.DMA((2,)),
            pltpu.SemaphoreType.DMA((2,)),
        ],
    )
    return pl.pallas_call(
        _experts_kernel,
        grid_spec=grid_spec,
        out_shape=jax.ShapeDtypeStruct((TOP_K * SLAB * TOKEN_ROWS, LANES), F32),
        compiler_params=pltpu.CompilerParams(
            dimension_semantics=("arbitrary",), vmem_limit_bytes=VMEM_LIMIT),
        name="experts",
    )(inv, be, nb, hn_all, w_up, b_up, w_down, b_down)


def _combine_kernel(ys_ref, gate_ref, h_ref, nfin_ref, out_ref):
    gates = gate_ref[...]
    acc = h_ref[...]
    for k in range(TOP_K):
        acc = acc + gates[:, k:k + 1] * _load_token_tiles(ys_ref.at[k])
    out_ref[...] = _rmsnorm(acc, nfin_ref[...])


def _combine(ys, gates, h_all, nfin, rows, tile, first_block, name):
    return pl.pallas_call(
        _combine_kernel,
        grid=(rows // tile,),
        in_specs=[
            pl.BlockSpec((TOP_K, tile * TOKEN_ROWS, LANES), lambda i: (0, first_block + i, 0)),
            pl.BlockSpec((tile, SUBLANES), lambda i: (first_block + i, 0)),
            pl.BlockSpec((tile, D_MODEL), lambda i: (first_block + i, 0)),
            pl.BlockSpec((1, D_MODEL), lambda i: (0, 0)),
        ],
        out_specs=pl.BlockSpec((tile, D_MODEL), lambda i: (i, 0)),
        out_shape=jax.ShapeDtypeStruct((rows, D_MODEL), F32),
        compiler_params=pltpu.CompilerParams(
            dimension_semantics=("arbitrary",), vmem_limit_bytes=VMEM_LIMIT),
        name=name,
    )(ys, gates, h_all, nfin)


def _rope_tables(positions):
    half = RET_HD // 2
    inv = ROPE_BASE ** (-jnp.arange(half, dtype=F32) / half)
    ang = positions.astype(F32)[:, None] * inv[None, :]
    cos, sin = jnp.cos(ang), jnp.sin(ang)
    return jnp.concatenate([cos, cos], axis=-1), jnp.concatenate([-sin, sin], axis=-1)


def _decay_tables(log_gamma, c):
    pos = jnp.arange(c, dtype=F32)
    diff = pos[:, None] - pos[None, :]
    causal = diff >= 0
    lg = log_gamma[:, None, None]
    decay = jnp.where(causal, jnp.exp(lg * jnp.where(causal, diff, 0.0)), 0.0)
    xi = jnp.exp(log_gamma[:, None] * (pos[None, :] + 1.0))
    zeta = jnp.exp(log_gamma[:, None] * (c - 1.0 - pos[None, :]))
    wide = lambda t: jnp.broadcast_to(t[..., None], t.shape + (RET_HD,))
    gl = jnp.broadcast_to(jnp.exp(log_gamma * c)[:, None, None], (RET_HEADS, 1, RET_HD))
    return decay, wide(xi), wide(zeta), gl


def kernel(x_prompt, x_sample, state_conv, state_ret, norm_mix, w_in, conv_w, ret_norm, w_out, norm_ffn,
           w_router, b_router, w_up, b_up, w_down, b_down, norm_final):
    assert norm_mix.shape[0] == 1, "single trunk layer"
    nmix = norm_mix[0][None, :]
    nffn = norm_ffn[0][None, :]
    nfin = norm_final[None, :]
    rnorm = ret_norm[0][None, :]
    win = w_in[0].astype(BF16)
    wout = w_out[0].astype(BF16)
    wr = w_router[0]
    wrhi = wr.astype(BF16)
    wrlo = (wr - wrhi.astype(F32)).astype(BF16)
    br = b_router[0][None, :]
    convw = conv_w[0]

    log_gamma = jnp.log(1.0 - 2.0 ** (-5.0 - jnp.arange(RET_HEADS, dtype=F32)))
    decay, xi, zeta, gl = _decay_tables(log_gamma, TILE_L)
    cos_p, sin_p = _rope_tables(jnp.arange(SEQ, dtype=jnp.int32))
    cos_s, sin_s = _rope_tables(PAST_LEN + jnp.arange(1, dtype=jnp.int32))
    gamma1 = jnp.broadcast_to(jnp.exp(log_gamma)[:, None, None], (RET_HEADS, 1, RET_HD))

    h_s, hn_s, idx_s, gate_s, cnt_s, conv_s, ret_s = _mixer_sample(
        x_sample.reshape(DEC_BATCH, D_MODEL),
        state_conv[0].reshape(DEC_BATCH, (CONV_K - 1) * CONV_W),
        state_ret[0], nmix, win, convw, cos_s, sin_s, gamma1, rnorm, wout, nffn, wrhi, wrlo, br)

    h_all, hn_all, idx_all, gate_all, cnt, conv_p, ret_p = _mixer_prompt(
        x_prompt, nmix, win, convw, cos_p, sin_p, decay, xi, zeta, gl, rnorm, wout, nffn, wrhi, wrlo, br,
        h_s, hn_s, idx_s, gate_s, cnt_s)

    idx_lanes = idx_all[0:T_ALL, 0:TOP_K].T
    cnt_i = cnt[0].astype(I32)
    cnt_col = jnp.broadcast_to(cnt_i[:, None], (N_EXPERTS, LANES))
    dest, be, nb, pstart_col = _route_rank(idx_lanes, cnt_col)
    inv = _route_invert(dest.reshape(DEST_ROWS * LANES), cnt_i, pstart_col[:, 0])

    ys = _experts(inv, be[0, 0:N_BLOCKS], nb[0, 0:1], hn_all,
                  w_up[0], b_up[0][:, None, :], w_down[0], b_down[0][:, None, :])
    ys = ys.reshape(TOP_K, SLAB * TOKEN_ROWS, LANES)

    y_prompt = _combine(ys, gate_all, h_all, nfin, T_PROMPT, FINAL_TILE, 0, "combine_prompt")
    y_sample = _combine(ys, gate_all, h_all, nfin, DEC_BATCH, DEC_BATCH, T_PROMPT // DEC_BATCH,
                        "combine_sample")

    return (y_prompt.reshape(BATCH, SEQ, D_MODEL),
            y_sample.reshape(DEC_BATCH, 1, D_MODEL),
            conv_p,
            ret_p,
            conv_s.reshape(1, DEC_BATCH, CONV_K - 1, CONV_W),
            ret_s.reshape(1, DEC_BATCH, RET_HEADS, RET_HD, RET_HD))
```

```python
import functools

import jax
import jax.numpy as jnp
from jax import lax
from jax.experimental import pallas as pl
from jax.experimental.pallas import tpu as pltpu

F32 = jnp.float32
BF16 = jnp.bfloat16
I32 = jnp.int32

D_MODEL = 1024
BATCH = 8
SEQ = 2048
DEC_BATCH = 128
PAST_LEN = 16384
CONV_W = 512
CONV_K = 3
RET_W = 512
RET_HEADS = 4
RET_HD = 128
ROPE_BASE = 10000.0
PROJ_W = 3 * CONV_W + 4 * RET_W
N_EXPERTS = 32
TOP_K = 4
D_FF = D_MODEL
SWIGLU_LIMIT = 7.0
SWIGLU_ALPHA = 1.702
RMS_EPS = 1e-6
GN_EPS = 1e-5

T_PROMPT = BATCH * SEQ
T_ALL = T_PROMPT + DEC_BATCH
N_ASSIGN = T_ALL * TOP_K

LANES = 128
SUBLANES = 8
TOKEN_ROWS = D_MODEL // LANES
TILE_L = 256
N_TILES = T_PROMPT // TILE_L
T_BUF = (N_TILES + 1) * TILE_L
BLOCK_M = 256
BLOCK_SHIFT = 8
N_BLOCKS = N_ASSIGN // BLOCK_M + N_EXPERTS
BLOCK_LANES = 3 * LANES
DUMP_ROWS = (N_EXPERTS + 1) * BLOCK_M // TOP_K
SLAB = T_ALL + DUMP_ROWS
SRC_BITS = 15
N_CHUNKS = T_ALL // LANES
DEST_ROWS = 520
FINAL_TILE = 512
VMEM_LIMIT = 56 * 1024 * 1024

assert N_ASSIGN % BLOCK_M == 0 and N_BLOCKS <= BLOCK_LANES and (1 << BLOCK_SHIFT) == BLOCK_M
assert T_ALL % LANES == 0 and TOP_K * N_CHUNKS <= DEST_ROWS
assert T_BUF <= (1 << SRC_BITS) and TOP_K * SLAB <= (1 << (32 - SRC_BITS)) and T_ALL + LANES <= T_BUF


def _rmsnorm(x, g):
    ms = jnp.mean(x * x, axis=-1, keepdims=True)
    return x * lax.rsqrt(ms + RMS_EPS) * g


def _store_token_tiles(ref, val):
    rows = val.shape[0]
    for s in range(TOKEN_ROWS):
        ref[pl.ds(s, rows, stride=TOKEN_ROWS), :] = val[:, s * LANES:(s + 1) * LANES]


def _load_token_tiles(ref):
    rows = ref.shape[0] // TOKEN_ROWS
    return jnp.concatenate([ref[pl.ds(s, rows, stride=TOKEN_ROWS), :] for s in range(TOKEN_ROWS)], axis=1)


def _dot(a, b):
    return jnp.dot(a, b, preferred_element_type=F32)


def _dot_nt(a, b):
    return lax.dot_general(a, b, (((1,), (1,)), ((), ())), preferred_element_type=F32)


def _dot_tn(a, b):
    return lax.dot_general(a, b, (((0,), (0,)), ((), ())), preferred_element_type=F32)


def _rotary(x, cos2, sin2):
    return x * cos2 + pltpu.roll(x, RET_HD // 2, 1) * sin2


def _group_norm_gate(o, g, w):
    mu = jnp.mean(o, axis=-1, keepdims=True)
    d = o - mu
    var = jnp.mean(d * d, axis=-1, keepdims=True)
    return (g * jax.nn.sigmoid(g)) * (d * lax.rsqrt(var + GN_EPS) * w)


def _route(hn, wr_hi, wr_lo, b_router, gate_ref, idx_ref):
    rows = hn.shape[0]
    hi = hn.astype(BF16)
    lo = (hn - hi.astype(F32)).astype(BF16)
    logits = _dot(hi, wr_hi) + _dot(lo, wr_hi) + _dot(hi, wr_lo) + b_router
    lane = lax.broadcasted_iota(I32, (rows, N_EXPERTS), 1)
    vals, sels = [], []
    onehot = jnp.zeros((rows, N_EXPERTS), F32)
    for _ in range(TOP_K):
        m = jnp.max(logits, axis=-1, keepdims=True)
        sel = jnp.min(jnp.where(logits == m, lane, N_EXPERTS), axis=-1, keepdims=True)
        hit = lane == sel
        onehot = onehot + hit.astype(F32)
        logits = jnp.where(hit, -jnp.inf, logits)
        vals.append(m)
        sels.append(sel)
    ex = [jnp.exp(v - vals[0]) for v in vals]
    den = ex[0] + ex[1] + ex[2] + ex[3]
    zero = jnp.zeros((rows, SUBLANES - TOP_K), F32)
    gate_ref[...] = jnp.concatenate([e / den for e in ex] + [zero], axis=1)
    idx_ref[...] = jnp.concatenate(sels + [zero.astype(I32)], axis=1)
    return jnp.sum(onehot.reshape(rows // SUBLANES, SUBLANES, N_EXPERTS), axis=0)


def _mixer_prompt_kernel(x_ref, nmix_ref, win_ref, convw_ref, cos_ref, sin_ref, decay_ref, xi_ref,
                         zeta_ref, gl_ref, rnorm_ref, wout_ref, nffn_ref, wrhi_ref, wrlo_ref, br_ref,
                         hs_ref, hns_ref, idxs_ref, gates_ref, cnts_ref,
                         h_ref, hn_ref, idx_ref, gate_ref, cnt_ref, nconv_ref, nret_ref,
                         ubuf, sbuf, mixbuf):
    step = pl.program_id(0)

    @pl.when(step == 0)
    def _():
        cnt_ref[...] = cnts_ref[...]

    @pl.when(step < N_TILES)
    def _():
        _mixer_prompt_tile(step % (SEQ // TILE_L), x_ref, nmix_ref, win_ref, convw_ref, cos_ref, sin_ref,
                           decay_ref, xi_ref, zeta_ref, gl_ref, rnorm_ref, wout_ref, nffn_ref, wrhi_ref,
                           wrlo_ref, br_ref, h_ref, hn_ref, idx_ref, gate_ref, cnt_ref, nconv_ref, nret_ref,
                           ubuf, sbuf, mixbuf)

    @pl.when(step == N_TILES)
    def _():
        for dst, src in ((h_ref, hs_ref), (hn_ref, hns_ref), (idx_ref, idxs_ref), (gate_ref, gates_ref)):
            n_src, n_dst = src.shape[0], dst.shape[0]
            dst[0:n_src, :] = src[...]
            dst[n_src:n_dst, :] = jnp.zeros((n_dst - n_src, dst.shape[1]), dst.dtype)
        cnt = cnt_ref[...]
        cnt_ref[...] = jnp.broadcast_to(jnp.sum(cnt, axis=0, keepdims=True), cnt.shape)


def _mixer_prompt_tile(i, x_ref, nmix_ref, win_ref, convw_ref, cos_ref, sin_ref, decay_ref, xi_ref,
                       zeta_ref, gl_ref, rnorm_ref, wout_ref, nffn_ref, wrhi_ref, wrlo_ref, br_ref,
                       h_ref, hn_ref, idx_ref, gate_ref, cnt_ref, nconv_ref, nret_ref,
                       ubuf, sbuf, mixbuf):
    @pl.when(i == 0)
    def _():
        ubuf[0:SUBLANES, :] = jnp.zeros((SUBLANES, CONV_W), F32)
        sbuf[...] = jnp.zeros_like(sbuf)

    x = x_ref[0]
    xn = _rmsnorm(x, nmix_ref[...]).astype(BF16)

    pc = _dot(xn, win_ref[:, 0:3 * CONV_W])
    u = pc[:, 2 * CONV_W:3 * CONV_W] * pc[:, 0:CONV_W]
    ubuf[SUBLANES:SUBLANES + TILE_L, :] = u
    conv = (convw_ref[0:1, :] * ubuf[SUBLANES - 2:SUBLANES - 2 + TILE_L, :]
            + convw_ref[1:2, :] * ubuf[SUBLANES - 1:SUBLANES - 1 + TILE_L, :]
            + convw_ref[2:3, :] * u)
    mixbuf[:, 0:CONV_W] = (pc[:, CONV_W:2 * CONV_W] * conv).astype(BF16)
    nconv_ref[0, 0] = u[TILE_L - (CONV_K - 1):TILE_L, :]
    ubuf[0:SUBLANES, :] = u[TILE_L - SUBLANES:TILE_L, :]

    pr = _dot(xn, win_ref[:, 3 * CONV_W:PROJ_W])
    cos2 = cos_ref[...]
    sin2 = sin_ref[...]
    for hh in range(RET_HEADS):
        lo = hh * RET_HD
        q = pr[:, lo:lo + RET_HD]
        k = pr[:, RET_W + lo:RET_W + lo + RET_HD]
        v = pr[:, 2 * RET_W + lo:2 * RET_W + lo + RET_HD]
        g = pr[:, 3 * RET_W + lo:3 * RET_W + lo + RET_HD]
        qr = _rotary(q, cos2, sin2)
        kr = _rotary(k, cos2, sin2) * (RET_HD ** -0.5)
        vb = v.astype(BF16)
        scores = _dot_nt(qr.astype(BF16), kr.astype(BF16)) * decay_ref[hh]
        state = sbuf[hh]
        o = _dot(scores.astype(BF16), vb) + _dot((qr * xi_ref[hh]).astype(BF16), state.astype(BF16))
        sbuf[hh] = gl_ref[hh] * state + _dot_tn((kr * zeta_ref[hh]).astype(BF16), vb)
        r = _group_norm_gate(o, g, rnorm_ref[:, lo:lo + RET_HD])
        mixbuf[:, CONV_W + lo:CONV_W + lo + RET_HD] = r.astype(BF16)
    nret_ref[0, 0] = sbuf[...]

    h = x + _dot(mixbuf[...], wout_ref[...])
    h_ref[...] = h
    hn = _rmsnorm(h, nffn_ref[...])
    _store_token_tiles(hn_ref, hn)
    cnt_ref[...] += _route(hn, wrhi_ref[...], wrlo_ref[...], br_ref[...], gate_ref, idx_ref)


def _mixer_prompt(x, nmix, win, convw, cos2, sin2, decay, xi, zeta, gl, rnorm, wout, nffn, wrhi, wrlo, br,
                  h_s, hn_s, idx_s, gate_s, cnt_s):
    n_l = SEQ // TILE_L
    tile = lambda s: jnp.minimum(s, N_TILES - 1)
    row = lambda s: (s, 0)
    const2 = lambda s: (0, 0)
    const3 = lambda s: (0, 0, 0)
    return pl.pallas_call(
        _mixer_prompt_kernel,
        grid=(N_TILES + 1,),
        in_specs=[
            pl.BlockSpec((1, TILE_L, D_MODEL), lambda s: (tile(s) // n_l, tile(s) % n_l, 0)),
            pl.BlockSpec((1, D_MODEL), const2),
            pl.BlockSpec((D_MODEL, PROJ_W), const2),
            pl.BlockSpec((CONV_K, CONV_W), const2),
            pl.BlockSpec((TILE_L, RET_HD), lambda s: (tile(s) % n_l, 0)),
            pl.BlockSpec((TILE_L, RET_HD), lambda s: (tile(s) % n_l, 0)),
            pl.BlockSpec((RET_HEADS, TILE_L, TILE_L), const3),
            pl.BlockSpec((RET_HEADS, TILE_L, RET_HD), const3),
            pl.BlockSpec((RET_HEADS, TILE_L, RET_HD), const3),
            pl.BlockSpec((RET_HEADS, 1, RET_HD), const3),
            pl.BlockSpec((1, RET_W), const2),
            pl.BlockSpec((D_MODEL, D_MODEL), const2),
            pl.BlockSpec((1, D_MODEL), const2),
            pl.BlockSpec((D_MODEL, N_EXPERTS), const2),
            pl.BlockSpec((D_MODEL, N_EXPERTS), const2),
            pl.BlockSpec((1, N_EXPERTS), const2),
            pl.BlockSpec((DEC_BATCH, D_MODEL), const2),
            pl.BlockSpec((DEC_BATCH * TOKEN_ROWS, LANES), const2),
            pl.BlockSpec((DEC_BATCH, SUBLANES), const2),
            pl.BlockSpec((DEC_BATCH, SUBLANES), const2),
            pl.BlockSpec((SUBLANES, N_EXPERTS), const2),
        ],
        out_specs=[
            pl.BlockSpec((TILE_L, D_MODEL), row),
            pl.BlockSpec((TILE_L * TOKEN_ROWS, LANES), row),
            pl.BlockSpec((TILE_L, SUBLANES), row),
            pl.BlockSpec((TILE_L, SUBLANES), row),
            pl.BlockSpec((SUBLANES, N_EXPERTS), const2),
            pl.BlockSpec((1, 1, CONV_K - 1, CONV_W), lambda s: (0, tile(s) // n_l, 0, 0)),
            pl.BlockSpec((1, 1, RET_HEADS, RET_HD, RET_HD), lambda s: (0, tile(s) // n_l, 0, 0, 0)),
        ],
        out_shape=[
            jax.ShapeDtypeStruct((T_BUF, D_MODEL), F32),
            jax.ShapeDtypeStruct((T_BUF * TOKEN_ROWS, LANES), F32),
            jax.ShapeDtypeStruct((T_BUF, SUBLANES), I32),
            jax.ShapeDtypeStruct((T_BUF, SUBLANES), F32),
            jax.ShapeDtypeStruct((SUBLANES, N_EXPERTS), F32),
            jax.ShapeDtypeStruct((1, BATCH, CONV_K - 1, CONV_W), F32),
            jax.ShapeDtypeStruct((1, BATCH, RET_HEADS, RET_HD, RET_HD), F32),
        ],
        scratch_shapes=[
            pltpu.VMEM((TILE_L + SUBLANES, CONV_W), F32),
            pltpu.VMEM((RET_HEADS, RET_HD, RET_HD), F32),
            pltpu.VMEM((TILE_L, D_MODEL), BF16),
        ],
        compiler_params=pltpu.CompilerParams(
            dimension_semantics=("arbitrary",), vmem_limit_bytes=VMEM_LIMIT),
        name="mixer_prompt",
    )(x, nmix, win, convw, cos2, sin2, decay, xi, zeta, gl, rnorm, wout, nffn, wrhi, wrlo, br,
      h_s, hn_s, idx_s, gate_s, cnt_s)


SAMPLE_GROUP = 8
N_GROUPS = DEC_BATCH // SAMPLE_GROUP


def _mixer_sample_kernel(x_ref, sconv_ref, sret_ref, nmix_ref, win_ref, convw_ref, cos_ref, sin_ref,
                         gamma_ref, rnorm_ref, wout_ref, nffn_ref, wrhi_ref, wrlo_ref, br_ref,
                         h_ref, hn_ref, idx_ref, gate_ref, cnt_ref, nconv_ref, nret_ref,
                         qt_buf, kt_buf, v_buf, o_buf, g_buf, mixbuf):
    grp = pl.program_id(0)

    @pl.when(grp == 0)
    def _():
        x = x_ref[...]
        xn = _rmsnorm(x, nmix_ref[...]).astype(BF16)
        pc = _dot(xn, win_ref[:, 0:3 * CONV_W])
        u = pc[:, 2 * CONV_W:3 * CONV_W] * pc[:, 0:CONV_W]
        st = sconv_ref[...]
        conv = (convw_ref[0:1, :] * st[:, 0:CONV_W] + convw_ref[1:2, :] * st[:, CONV_W:2 * CONV_W]
                + convw_ref[2:3, :] * u)
        mixbuf[:, 0:CONV_W] = (pc[:, CONV_W:2 * CONV_W] * conv).astype(BF16)
        nconv_ref[:, 0:CONV_W] = st[:, CONV_W:2 * CONV_W]
        nconv_ref[:, CONV_W:2 * CONV_W] = u

        pr = _dot(xn, win_ref[:, 3 * CONV_W:PROJ_W])
        g_buf[...] = pr[:, 3 * RET_W:4 * RET_W]
        cos2 = cos_ref[...]
        sin2 = sin_ref[...]
        for hh in range(RET_HEADS):
            lo = hh * RET_HD
            qr = _rotary(pr[:, lo:lo + RET_HD], cos2, sin2)
            kr = _rotary(pr[:, RET_W + lo:RET_W + lo + RET_HD], cos2, sin2) * (RET_HD ** -0.5)
            v = pr[:, 2 * RET_W + lo:2 * RET_W + lo + RET_HD]
            v_buf[hh] = v
            o_buf[hh] = jnp.sum(qr * kr, axis=-1, keepdims=True) * v
            qt = qr.T
            kt = kr.T
            for gg in range(N_GROUPS):
                qt_buf[hh, gg] = qt[:, gg * SAMPLE_GROUP:(gg + 1) * SAMPLE_GROUP]
                kt_buf[hh, gg] = kt[:, gg * SAMPLE_GROUP:(gg + 1) * SAMPLE_GROUP]

    base = pl.multiple_of(grp * SAMPLE_GROUP, SAMPLE_GROUP)
    for hh in range(RET_HEADS):
        gam = gamma_ref[hh]
        for j in range(SAMPLE_GROUP):
            state = sret_ref[j, hh]
            qc = qt_buf[hh, grp, :, j:j + 1]
            kc = kt_buf[hh, grp, :, j:j + 1]
            vrow = v_buf[hh, pl.ds(base + j, 1), :]
            cross = jnp.sum((qc * gam) * state, axis=0, keepdims=True)
            o_buf[hh, pl.ds(base + j, 1), :] = o_buf[hh, pl.ds(base + j, 1), :] + cross
            nret_ref[j, hh] = gam * state + kc * vrow

    @pl.when(grp == N_GROUPS - 1)
    def _():
        for hh in range(RET_HEADS):
            lo = hh * RET_HD
            r = _group_norm_gate(o_buf[hh], g_buf[:, lo:lo + RET_HD], rnorm_ref[:, lo:lo + RET_HD])
            mixbuf[:, CONV_W + lo:CONV_W + lo + RET_HD] = r.astype(BF16)
        h = x_ref[...] + _dot(mixbuf[...], wout_ref[...])
        h_ref[...] = h
        hn = _rmsnorm(h, nffn_ref[...])
        _store_token_tiles(hn_ref, hn)
        cnt_ref[...] = _route(hn, wrhi_ref[...], wrlo_ref[...], br_ref[...], gate_ref, idx_ref)


def _mixer_sample(x, sconv, sret, nmix, win, convw, cos2, sin2, gamma, rnorm, wout, nffn, wrhi, wrlo, br):
    const2 = lambda g: (0, 0)
    const3 = lambda g: (0, 0, 0)
    return pl.pallas_call(
        _mixer_sample_kernel,
        grid=(N_GROUPS,),
        in_specs=[
            pl.BlockSpec((DEC_BATCH, D_MODEL), const2),
            pl.BlockSpec((DEC_BATCH, (CONV_K - 1) * CONV_W), const2),
            pl.BlockSpec((SAMPLE_GROUP, RET_HEADS, RET_HD, RET_HD), lambda g: (g, 0, 0, 0)),
            pl.BlockSpec((1, D_MODEL), const2),
            pl.BlockSpec((D_MODEL, PROJ_W), const2),
            pl.BlockSpec((CONV_K, CONV_W), const2),
            pl.BlockSpec((1, RET_HD), const2),
            pl.BlockSpec((1, RET_HD), const2),
            pl.BlockSpec((RET_HEADS, 1, RET_HD), const3),
            pl.BlockSpec((1, RET_W), const2),
            pl.BlockSpec((D_MODEL, D_MODEL), const2),
            pl.BlockSpec((1, D_MODEL), const2),
            pl.BlockSpec((D_MODEL, N_EXPERTS), const2),
            pl.BlockSpec((D_MODEL, N_EXPERTS), const2),
            pl.BlockSpec((1, N_EXPERTS), const2),
        ],
        out_specs=[
            pl.BlockSpec((DEC_BATCH, D_MODEL), const2),
            pl.BlockSpec((DEC_BATCH * TOKEN_ROWS, LANES), const2),
            pl.BlockSpec((DEC_BATCH, SUBLANES), const2),
            pl.BlockSpec((DEC_BATCH, SUBLANES), const2),
            pl.BlockSpec((SUBLANES, N_EXPERTS), const2),
            pl.BlockSpec((DEC_BATCH, (CONV_K - 1) * CONV_W), const2),
            pl.BlockSpec((SAMPLE_GROUP, RET_HEADS, RET_HD, RET_HD), lambda g: (g, 0, 0, 0)),
        ],
        out_shape=[
            jax.ShapeDtypeStruct((DEC_BATCH, D_MODEL), F32),
            jax.ShapeDtypeStruct((DEC_BATCH * TOKEN_ROWS, LANES), F32),
            jax.ShapeDtypeStruct((DEC_BATCH, SUBLANES), I32),
            jax.ShapeDtypeStruct((DEC_BATCH, SUBLANES), F32),
            jax.ShapeDtypeStruct((SUBLANES, N_EXPERTS), F32),
            jax.ShapeDtypeStruct((DEC_BATCH, (CONV_K - 1) * CONV_W), F32),
            jax.ShapeDtypeStruct((DEC_BATCH, RET_HEADS, RET_HD, RET_HD), F32),
        ],
        scratch_shapes=[
            pltpu.VMEM((RET_HEADS, N_GROUPS, RET_HD, SAMPLE_GROUP), F32),
            pltpu.VMEM((RET_HEADS, N_GROUPS, RET_HD, SAMPLE_GROUP), F32),
            pltpu.VMEM((RET_HEADS, DEC_BATCH, RET_HD), F32),
            pltpu.VMEM((RET_HEADS, DEC_BATCH, RET_HD), F32),
            pltpu.VMEM((DEC_BATCH, RET_W), F32),
            pltpu.VMEM((DEC_BATCH, D_MODEL), BF16),
        ],
        compiler_params=pltpu.CompilerParams(
            dimension_semantics=("arbitrary",), vmem_limit_bytes=VMEM_LIMIT),
        name="mixer_sample",
    )(x, sconv, sret, nmix, win, convw, cos2, sin2, gamma, rnorm, wout, nffn, wrhi, wrlo, br)


def _route_rank_kernel(idx_ref, cnt_ref, dest_ref, be_ref, nb_ref, pstart_ref):
    cnt = cnt_ref[...]
    padded = ((cnt + (BLOCK_M - 1)) >> BLOCK_SHIFT) << BLOCK_SHIFT
    sub = lax.broadcasted_iota(I32, (N_EXPERTS, LANES), 0)
    pstart = jnp.zeros((N_EXPERTS, LANES), I32)
    for e in range(N_EXPERTS - 1):
        pstart = pstart + jnp.where(sub > e, padded[e:e + 1, :], 0)
    pend = pstart + padded
    pstart_ref[...] = pstart
    nb = pend[N_EXPERTS - 1:N_EXPERTS, :] >> BLOCK_SHIFT
    nb_ref[...] = nb

    blk = lax.broadcasted_iota(I32, (N_EXPERTS, BLOCK_LANES), 1)
    pend_w = jnp.concatenate([pend] * (BLOCK_LANES // LANES), axis=1)
    be = jnp.sum((pend_w <= blk * BLOCK_M).astype(I32), axis=0, keepdims=True)
    be = jnp.minimum(be, N_EXPERTS - 1)
    used = blk[0:1, :] < jnp.concatenate([nb] * (BLOCK_LANES // LANES), axis=1)
    last = jnp.max(jnp.where(used, be, 0), axis=1, keepdims=True)
    be_ref[...] = jnp.where(used, be, last)

    dest_ref[...] = jnp.zeros_like(dest_ref)
    upper = (lax.broadcasted_iota(I32, (LANES, LANES), 0)
             < lax.broadcasted_iota(I32, (LANES, LANES), 1)).astype(BF16)
    pstart_f = pstart.astype(F32)

    carry = jnp.zeros((N_EXPERTS, LANES), F32)
    for k in range(TOP_K):
        def chunk(c, carry, k=k):
            ids = idx_ref[k:k + 1, pl.ds(pl.multiple_of(c * LANES, LANES), LANES)]
            hit = sub == ids
            hit_f = hit.astype(F32)
            before = _dot(hit_f.astype(BF16), upper)
            pos = jnp.sum(jnp.where(hit, before + carry + pstart_f, 0.0), axis=0, keepdims=True)
            dest_ref[pl.ds(k * N_CHUNKS + c, 1), :] = pos.astype(I32)
            return carry + jnp.sum(hit_f, axis=1, keepdims=True)

        carry = lax.fori_loop(0, N_CHUNKS, chunk, carry)


def _route_rank(idx_lanes, cnt_col):
    return pl.pallas_call(
        _route_rank_kernel,
        out_shape=[
            jax.ShapeDtypeStruct((DEST_ROWS, LANES), I32),
            jax.ShapeDtypeStruct((1, BLOCK_LANES), I32),
            jax.ShapeDtypeStruct((1, LANES), I32),
            jax.ShapeDtypeStruct((N_EXPERTS, LANES), I32),
        ],
        name="route_rank",
    )(idx_lanes, cnt_col)


INVERT_ROWS = DEST_ROWS // 5
INVERT_GROUP = 16


def _pack_row(src_token, dst_row):
    return (jnp.asarray(dst_row, I32) << SRC_BITS) | jnp.asarray(src_token, I32)


def _pad_entry(p):
    return _pack_row(T_ALL + (p & (LANES - 1)), (p & (TOP_K - 1)) * SLAB + T_ALL + (p >> 2))


def _route_invert_kernel(cnt_ref, pstart_ref, dest_hbm, inv_ref, dest_s, sem):
    def pad_expert(e, _):
        n = cnt_ref[e]
        first = pstart_ref[e] + n
        n_pad = (((n + (BLOCK_M - 1)) >> BLOCK_SHIFT) << BLOCK_SHIFT) - n

        def pad_row(r, _):
            inv_ref[BLOCK_M + first + r] = _pad_entry(e * BLOCK_M + r)
            return 0

        lax.fori_loop(0, n_pad, pad_row, 0)
        return 0

    part = pl.program_id(0)

    @pl.when(part == 0)
    def _():
        lax.fori_loop(0, N_EXPERTS, pad_expert, 0)

        def lead_row(r, _):
            inv_ref[r] = _pad_entry(N_EXPERTS * BLOCK_M + r)
            return 0

        lax.fori_loop(0, BLOCK_M, lead_row, 0)

    row0 = part * INVERT_ROWS
    copy = pltpu.make_async_copy(
        dest_hbm.at[pl.ds(pl.multiple_of(row0 * LANES, LANES), INVERT_ROWS * LANES)], dest_s, sem)
    copy.start()
    copy.wait()

    def chunk(rr, _):
        row = row0 + rr
        k = row // N_CHUNKS
        tok0 = (row - k * N_CHUNKS) * LANES
        val0 = _pack_row(tok0, k * SLAB + tok0)
        base = rr * LANES
        for g in range(0, LANES, INVERT_GROUP):
            dests = [dest_s[base + g + i] for i in range(INVERT_GROUP)]
            for i, d in enumerate(dests):
                inv_ref[BLOCK_M + d] = val0 + (g + i) * _pack_row(1, 1)
        return 0

    lax.fori_loop(0, jnp.minimum(INVERT_ROWS, TOP_K * N_CHUNKS - row0), chunk, 0)


def _route_invert(dest, cnt, pstart):
    smem = pl.BlockSpec(memory_space=pltpu.SMEM)
    return pl.pallas_call(
        _route_invert_kernel,
        grid=(DEST_ROWS // INVERT_ROWS,),
        in_specs=[smem, smem, pl.BlockSpec(memory_space=pl.ANY)],
        out_specs=smem,
        out_shape=jax.ShapeDtypeStruct(((N_BLOCKS + 1) * BLOCK_M,), I32),
        scratch_shapes=[pltpu.SMEM((INVERT_ROWS * LANES,), I32), pltpu.SemaphoreType.DMA(())],
        compiler_params=pltpu.CompilerParams(dimension_semantics=("arbitrary",)),
        name="route_invert",
    )(cnt, pstart, dest)


X_SLOTS = 4


def _experts_kernel(inv_ref, be_ref, nb_ref, hn_hbm, wup_ref, bup_ref, wdn_ref, bdn_ref, y_hbm,
                    xbuf, obuf, actbuf, wup_b, wdn_b, sem_in, sem_out):
    j = pl.program_id(0)
    nb = nb_ref[0]

    def token_rows(t):
        if isinstance(t, int):
            return pl.ds(t * TOKEN_ROWS, TOKEN_ROWS)
        return pl.ds(pl.multiple_of(t * TOKEN_ROWS, TOKEN_ROWS), TOKEN_ROWS)

    def gather_rows(blk, s):
        base = (blk + 1) * BLOCK_M
        for r in range(BLOCK_M):
            tok = inv_ref[base + r] & ((1 << SRC_BITS) - 1)
            pltpu.make_async_copy(hn_hbm.at[token_rows(tok)], xbuf.at[s, token_rows(r)], sem_in.at[s]).start()

    def scatter_rows(blk, s):
        base = (blk + 1) * BLOCK_M
        for r in range(BLOCK_M):
            row = lax.shift_right_logical(inv_ref[base + r], SRC_BITS)
            pltpu.make_async_copy(obuf.at[s, token_rows(r)], y_hbm.at[token_rows(row)], sem_out.at[s]).start()

    def wait_gather(s):
        pltpu.make_async_copy(hn_hbm.at[pl.ds(0, BLOCK_M * TOKEN_ROWS)], xbuf.at[s], sem_in.at[s]).wait()

    def wait_scatter(s):
        pltpu.make_async_copy(obuf.at[s], y_hbm.at[pl.ds(0, BLOCK_M * TOKEN_ROWS)], sem_out.at[s]).wait()

    @pl.when(j == 0)
    def _():
        gather_rows(0, 0)
        gather_rows(jnp.minimum(1, nb - 1), 1)
        obuf[1] = jnp.zeros(obuf.shape[1:], F32)

    @pl.when((j < nb) & ((j == 0) | (be_ref[j] != be_ref[jnp.maximum(j - 1, 0)])))
    def _():
        wup_b[...] = wup_ref[0].astype(BF16)
        wdn_b[...] = wdn_ref[0].astype(BF16)

    def step(x_slot):
        o_slot = x_slot % 2

        wait_gather(x_slot)
        scatter_rows(j - 1, 1 - o_slot)
        x = _load_token_tiles(xbuf.at[x_slot]).astype(BF16)
        hmid = _dot(x, wup_b[...]) + bup_ref[0]
        h_glu = jnp.minimum(hmid[:, 0:D_FF], SWIGLU_LIMIT)
        h_lin = jnp.clip(hmid[:, D_FF:2 * D_FF], -SWIGLU_LIMIT, SWIGLU_LIMIT)
        actbuf[...] = (h_glu * jax.nn.sigmoid(SWIGLU_ALPHA * h_glu) * (h_lin + 1.0)).astype(BF16)

        @pl.when(j >= 1)
        def _():
            wait_scatter(o_slot)

        gather_rows(jnp.minimum(j + 2, nb - 1), (x_slot + 2) % X_SLOTS)
        _store_token_tiles(obuf.at[o_slot], _dot(actbuf[...], wdn_b[...]) + bdn_ref[0])

        @pl.when(j == nb - 1)
        def _():
            scatter_rows(j, o_slot)
            wait_scatter(1 - o_slot)
            wait_scatter(o_slot)
            wait_gather((x_slot + 1) % X_SLOTS)
            wait_gather((x_slot + 2) % X_SLOTS)

    for x_slot in range(X_SLOTS):
        pl.when((j < nb) & (j % X_SLOTS == x_slot))(functools.partial(step, x_slot))


def _experts(inv, be, nb, hn_all, w_up, b_up, w_down, b_down):
    grid_spec = pltpu.PrefetchScalarGridSpec(
        num_scalar_prefetch=3,
        grid=(N_BLOCKS,),
        in_specs=[
            pl.BlockSpec(memory_space=pl.ANY),
            pl.BlockSpec((1, D_MODEL, 2 * D_FF), lambda j, inv, be, nb: (be[j], 0, 0)),
            pl.BlockSpec((1, 1, 2 * D_FF), lambda j, inv, be, nb: (be[j], 0, 0)),
            pl.BlockSpec((1, D_FF, D_MODEL), lambda j, inv, be, nb: (be[j], 0, 0)),
            pl.BlockSpec((1, 1, D_MODEL), lambda j, inv, be, nb: (be[j], 0, 0)),
        ],
        out_specs=pl.BlockSpec(memory_space=pl.ANY),
        scratch_shapes=[
            pltpu.VMEM((X_SLOTS, BLOCK_M * TOKEN_ROWS, LANES), F32),
            pltpu.VMEM((2, BLOCK_M * TOKEN_ROWS, LANES), F32),
            pltpu.VMEM((BLOCK_M, D_FF), BF16),
            pltpu.VMEM((D_MODEL, 2 * D_FF), BF16),
            pltpu.VMEM((D_FF, D_MODEL), BF16),
            pltpu.SemaphoreType.DMA((X_SLOTS,)),
            pltpu.SemaphoreType.DMA((2,)),
        ],
    )
    return pl.pallas_call(
        _experts_kernel,
        grid_spec=grid_spec,
        out_shape=jax.ShapeDtypeStruct((TOP_K * SLAB * TOKEN_ROWS, LANES), F32),
        compiler_params=pltpu.CompilerParams(
            dimension_semantics=("arbitrary",), vmem_limit_bytes=VMEM_LIMIT),
        name="experts",
    )(inv, be, nb, hn_all, w_up, b_up, w_down, b_down)


def _combine_kernel(ys_ref, gate_ref, h_ref, nfin_ref, out_ref):
    gates = gate_ref[...]
    acc = h_ref[...]
    for k in range(TOP_K):
        acc = acc + gates[:, k:k + 1] * _load_token_tiles(ys_ref.at[k])
    out_ref[...] = _rmsnorm(acc, nfin_ref[...])


def _combine(ys, gates, h_all, nfin, rows, tile, first_block, name):
    return pl.pallas_call(
        _combine_kernel,
        grid=(rows // tile,),
        in_specs=[
            pl.BlockSpec((TOP_K, tile * TOKEN_ROWS, LANES), lambda i: (0, first_block + i, 0)),
            pl.BlockSpec((tile, SUBLANES), lambda i: (first_block + i, 0)),
            pl.BlockSpec((tile, D_MODEL), lambda i: (first_block + i, 0)),
            pl.BlockSpec((1, D_MODEL), lambda i: (0, 0)),
        ],
        out_specs=pl.BlockSpec((tile, D_MODEL), lambda i: (i, 0)),
        out_shape=jax.ShapeDtypeStruct((rows, D_MODEL), F32),
        compiler_params=pltpu.CompilerParams(
            dimension_semantics=("arbitrary",), vmem_limit_bytes=VMEM_LIMIT),
        name=name,
    )(ys, gates, h_all, nfin)


def _rope_tables(positions):
    half = RET_HD // 2
    inv = ROPE_BASE ** (-jnp.arange(half, dtype=F32) / half)
    ang = positions.astype(F32)[:, None] * inv[None, :]
    cos, sin = jnp.cos(ang), jnp.sin(ang)
    return jnp.concatenate([cos, cos], axis=-1), jnp.concatenate([-sin, sin], axis=-1)


def _decay_tables(log_gamma, c):
    pos = jnp.arange(c, dtype=F32)
    diff = pos[:, None] - pos[None, :]
    causal = diff >= 0
    lg = log_gamma[:, None, None]
    decay = jnp.where(causal, jnp.exp(lg * jnp.where(causal, diff, 0.0)), 0.0)
    xi = jnp.exp(log_gamma[:, None] * (pos[None, :] + 1.0))
    zeta = jnp.exp(log_gamma[:, None] * (c - 1.0 - pos[None, :]))
    wide = lambda t: jnp.broadcast_to(t[..., None], t.shape + (RET_HD,))
    gl = jnp.broadcast_to(jnp.exp(log_gamma * c)[:, None, None], (RET_HEADS, 1, RET_HD))
    return decay, wide(xi), wide(zeta), gl


def kernel(x_prompt, x_sample, state_conv, state_ret, norm_mix, w_in, conv_w, ret_norm, w_out, norm_ffn,
           w_router, b_router, w_up, b_up, w_down, b_down, norm_final):
    assert norm_mix.shape[0] == 1, "single trunk layer"
    nmix = norm_mix[0][None, :]
    nffn = norm_ffn[0][None, :]
    nfin = norm_final[None, :]
    rnorm = ret_norm[0][None, :]
    win = w_in[0].astype(BF16)
    wout = w_out[0].astype(BF16)
    wr = w_router[0]
    wrhi = wr.astype(BF16)
    wrlo = (wr - wrhi.astype(F32)).astype(BF16)
    br = b_router[0][None, :]
    convw = conv_w[0]

    log_gamma = jnp.log(1.0 - 2.0 ** (-5.0 - jnp.arange(RET_HEADS, dtype=F32)))
    decay, xi, zeta, gl = _decay_tables(log_gamma, TILE_L)
    cos_p, sin_p = _rope_tables(jnp.arange(SEQ, dtype=jnp.int32))
    cos_s, sin_s = _rope_tables(PAST_LEN + jnp.arange(1, dtype=jnp.int32))
    gamma1 = jnp.broadcast_to(jnp.exp(log_gamma)[:, None, None], (RET_HEADS, 1, RET_HD))

    h_s, hn_s, idx_s, gate_s, cnt_s, conv_s, ret_s = _mixer_sample(
        x_sample.reshape(DEC_BATCH, D_MODEL),
        state_conv[0].reshape(DEC_BATCH, (CONV_K - 1) * CONV_W),
        state_ret[0], nmix, win, convw, cos_s, sin_s, gamma1, rnorm, wout, nffn, wrhi, wrlo, br)

    h_all, hn_all, idx_all, gate_all, cnt, conv_p, ret_p = _mixer_prompt(
        x_prompt, nmix, win, convw, cos_p, sin_p, decay, xi, zeta, gl, rnorm, wout, nffn, wrhi, wrlo, br,
        h_s, hn_s, idx_s, gate_s, cnt_s)

    idx_lanes = idx_all[0:T_ALL, 0:TOP_K].T
    cnt_i = cnt[0].astype(I32)
    cnt_col = jnp.broadcast_to(cnt_i[:, None], (N_EXPERTS, LANES))
    dest, be, nb, pstart_col = _route_rank(idx_lanes, cnt_col)
    inv = _route_invert(dest.reshape(DEST_ROWS * LANES), cnt_i, pstart_col[:, 0])

    ys = _experts(inv, be[0, 0:N_BLOCKS], nb[0, 0:1], hn_all,
                  w_up[0], b_up[0][:, None, :], w_down[0], b_down[0][:, None, :])
    ys = ys.reshape(TOP_K, SLAB * TOKEN_ROWS, LANES)

    y_prompt = _combine(ys, gate_all, h_all, nfin, T_PROMPT, FINAL_TILE, 0, "combine_prompt")
    y_sample = _combine(ys, gate_all, h_all, nfin, DEC_BATCH, DEC_BATCH, T_PROMPT // DEC_BATCH,
                        "combine_sample")

    return (y_prompt.reshape(BATCH, SEQ, D_MODEL),
            y_sample.reshape(DEC_BATCH, 1, D_MODEL),
            conv_p,
            ret_p,
            conv_s.reshape(1, DEC_BATCH, CONV_K - 1, CONV_W),
            ret_s.reshape(1, DEC_BATCH, RET_HEADS, RET_HD, RET_HD))
```

```python
import functools

import jax
import jax.numpy as jnp
from jax import lax
from jax.experimental import pallas as pl
from jax.experimental.pallas import tpu as pltpu

F32 = jnp.float32
BF16 = jnp.bfloat16
I32 = jnp.int32

D_MODEL = 1024
BATCH = 8
SEQ = 2048
DEC_BATCH = 128
PAST_LEN = 16384
CONV_W = 512
CONV_K = 3
RET_W = 512
RET_HEADS = 4
RET_HD = 128
ROPE_BASE = 10000.0
PROJ_W = 3 * CONV_W + 4 * RET_W
N_EXPERTS = 32
TOP_K = 4
D_FF = D_MODEL
SWIGLU_LIMIT = 7.0
SWIGLU_ALPHA = 1.702
RMS_EPS = 1e-6
GN_EPS = 1e-5

T_PROMPT = BATCH * SEQ
T_ALL = T_PROMPT + DEC_BATCH
N_ASSIGN = T_ALL * TOP_K

LANES = 128
SUBLANES = 8
TOKEN_ROWS = D_MODEL // LANES
TILE_L = 256
N_TILES = T_PROMPT // TILE_L
T_BUF = (N_TILES + 1) * TILE_L
BLOCK_M = 256
BLOCK_SHIFT = 8
N_BLOCKS = N_ASSIGN // BLOCK_M + N_EXPERTS
BLOCK_LANES = 3 * LANES
DUMP_ROWS = (N_EXPERTS + 1) * BLOCK_M // TOP_K
SLAB = T_ALL + DUMP_ROWS
SRC_BITS = 15
N_CHUNKS = T_ALL // LANES
DEST_ROWS = 520
FINAL_TILE = 512
VMEM_LIMIT = 56 * 1024 * 1024

assert N_ASSIGN % BLOCK_M == 0 and N_BLOCKS <= BLOCK_LANES and (1 << BLOCK_SHIFT) == BLOCK_M
assert T_ALL % LANES == 0 and TOP_K * N_CHUNKS <= DEST_ROWS
assert T_BUF <= (1 << SRC_BITS) and TOP_K * SLAB <= (1 << (32 - SRC_BITS)) and T_ALL + LANES <= T_BUF


def _rmsnorm(x, g):
    ms = jnp.mean(x * x, axis=-1, keepdims=True)
    return x * lax.rsqrt(ms + RMS_EPS) * g


def _store_token_tiles(ref, val):
    rows = val.shape[0]
    for s in range(TOKEN_ROWS):
        ref[pl.ds(s, rows, stride=TOKEN_ROWS), :] = val[:, s * LANES:(s + 1) * LANES]


def _load_token_tiles(ref):
    rows = ref.shape[0] // TOKEN_ROWS
    return jnp.concatenate([ref[pl.ds(s, rows, stride=TOKEN_ROWS), :] for s in range(TOKEN_ROWS)], axis=1)


def _dot(a, b):
    return jnp.dot(a, b, preferred_element_type=F32)


def _dot_nt(a, b):
    return lax.dot_general(a, b, (((1,), (1,)), ((), ())), preferred_element_type=F32)


def _dot_tn(a, b):
    return lax.dot_general(a, b, (((0,), (0,)), ((), ())), preferred_element_type=F32)


def _rotary(x, cos2, sin2):
    return x * cos2 + pltpu.roll(x, RET_HD // 2, 1) * sin2


def _group_norm_gate(o, g, w):
    mu = jnp.mean(o, axis=-1, keepdims=True)
    d = o - mu
    var = jnp.mean(d * d, axis=-1, keepdims=True)
    return (g * jax.nn.sigmoid(g)) * (d * lax.rsqrt(var + GN_EPS) * w)


def _route(hn, wr_hi, wr_lo, b_router, gate_ref, idx_ref):
    rows = hn.shape[0]
    hi = hn.astype(BF16)
    lo = (hn - hi.astype(F32)).astype(BF16)
    logits = _dot(hi, wr_hi) + _dot(lo, wr_hi) + _dot(hi, wr_lo) + b_router
    lane = lax.broadcasted_iota(I32, (rows, N_EXPERTS), 1)
    vals, sels = [], []
    onehot = jnp.zeros((rows, N_EXPERTS), F32)
    for _ in range(TOP_K):
        m = jnp.max(logits, axis=-1, keepdims=True)
        sel = jnp.min(jnp.where(logits == m, lane, N_EXPERTS), axis=-1, keepdims=True)
        hit = lane == sel
        onehot = onehot + hit.astype(F32)
        logits = jnp.where(hit, -jnp.inf, logits)
        vals.append(m)
        sels.append(sel)
    ex = [jnp.exp(v - vals[0]) for v in vals]
    den = ex[0] + ex[1] + ex[2] + ex[3]
    zero = jnp.zeros((rows, SUBLANES - TOP_K), F32)
    gate_ref[...] = jnp.concatenate([e / den for e in ex] + [zero], axis=1)
    idx_ref[...] = jnp.concatenate(sels + [zero.astype(I32)], axis=1)
    return jnp.sum(onehot.reshape(rows // SUBLANES, SUBLANES, N_EXPERTS), axis=0)


def _mixer_prompt_kernel(x_ref, nmix_ref, win_ref, convw_ref, cos_ref, sin_ref, decay_ref, xi_ref,
                         zeta_ref, gl_ref, rnorm_ref, wout_ref, nffn_ref, wrhi_ref, wrlo_ref, br_ref,
                         hs_ref, hns_ref, idxs_ref, gates_ref, cnts_ref,
                         h_ref, hn_ref, idx_ref, gate_ref, cnt_ref, nconv_ref, nret_ref,
                         ubuf, sbuf, mixbuf):
    step = pl.program_id(0)

    @pl.when(step == 0)
    def _():
        cnt_ref[...] = cnts_ref[...]

    @pl.when(step < N_TILES)
    def _():
        _mixer_prompt_tile(step % (SEQ // TILE_L), x_ref, nmix_ref, win_ref, convw_ref, cos_ref, sin_ref,
                           decay_ref, xi_ref, zeta_ref, gl_ref, rnorm_ref, wout_ref, nffn_ref, wrhi_ref,
                           wrlo_ref, br_ref, h_ref, hn_ref, idx_ref, gate_ref, cnt_ref, nconv_ref, nret_ref,
                           ubuf, sbuf, mixbuf)

    @pl.when(step == N_TILES)
    def _():
        for dst, src in ((h_ref, hs_ref), (hn_ref, hns_ref), (idx_ref, idxs_ref), (gate_ref, gates_ref)):
            n_src, n_dst = src.shape[0], dst.shape[0]
            dst[0:n_src, :] = src[...]
            dst[n_src:n_dst, :] = jnp.zeros((n_dst - n_src, dst.shape[1]), dst.dtype)
        cnt = cnt_ref[...]
        cnt_ref[...] = jnp.broadcast_to(jnp.sum(cnt, axis=0, keepdims=True), cnt.shape)


def _mixer_prompt_tile(i, x_ref, nmix_ref, win_ref, convw_ref, cos_ref, sin_ref, decay_ref, xi_ref,
                       zeta_ref, gl_ref, rnorm_ref, wout_ref, nffn_ref, wrhi_ref, wrlo_ref, br_ref,
                       h_ref, hn_ref, idx_ref, gate_ref, cnt_ref, nconv_ref, nret_ref,
                       ubuf, sbuf, mixbuf):
    @pl.when(i == 0)
    def _():
        ubuf[0:SUBLANES, :] = jnp.zeros((SUBLANES, CONV_W), F32)
        sbuf[...] = jnp.zeros_like(sbuf)

    x = x_ref[0]
    xn = _rmsnorm(x, nmix_ref[...]).astype(BF16)

    pc = _dot(xn, win_ref[:, 0:3 * CONV_W])
    u = pc[:, 2 * CONV_W:3 * CONV_W] * pc[:, 0:CONV_W]
    ubuf[SUBLANES:SUBLANES + TILE_L, :] = u
    conv = (convw_ref[0:1, :] * ubuf[SUBLANES - 2:SUBLANES - 2 + TILE_L, :]
            + convw_ref[1:2, :] * ubuf[SUBLANES - 1:SUBLANES - 1 + TILE_L, :]
            + convw_ref[2:3, :] * u)
    mixbuf[:, 0:CONV_W] = (pc[:, CONV_W:2 * CONV_W] * conv).astype(BF16)
    nconv_ref[0, 0] = u[TILE_L - (CONV_K - 1):TILE_L, :]
    ubuf[0:SUBLANES, :] = u[TILE_L - SUBLANES:TILE_L, :]

    pr = _dot(xn, win_ref[:, 3 * CONV_W:PROJ_W])
    cos2 = cos_ref[...]
    sin2 = sin_ref[...]
    for hh in range(RET_HEADS):
        lo = hh * RET_HD
        q = pr[:, lo:lo + RET_HD]
        k = pr[:, RET_W + lo:RET_W + lo + RET_HD]
        v = pr[:, 2 * RET_W + lo:2 * RET_W + lo + RET_HD]
        g = pr[:, 3 * RET_W + lo:3 * RET_W + lo + RET_HD]
        qr = _rotary(q, cos2, sin2)
        kr = _rotary(k, cos2, sin2) * (RET_HD ** -0.5)
        vb = v.astype(BF16)
        scores = _dot_nt(qr.astype(BF16), kr.astype(BF16)) * decay_ref[hh]
        state = sbuf[hh]
        o = _dot(scores.astype(BF16), vb) + _dot((qr * xi_ref[hh]).astype(BF16), state.astype(BF16))
        sbuf[hh] = gl_ref[hh] * state + _dot_tn((kr * zeta_ref[hh]).astype(BF16), vb)
        r = _group_norm_gate(o, g, rnorm_ref[:, lo:lo + RET_HD])
        mixbuf[:, CONV_W + lo:CONV_W + lo + RET_HD] = r.astype(BF16)
    nret_ref[0, 0] = sbuf[...]

    h = x + _dot(mixbuf[...], wout_ref[...])
    h_ref[...] = h
    hn = _rmsnorm(h, nffn_ref[...])
    _store_token_tiles(hn_ref, hn)
    cnt_ref[...] += _route(hn, wrhi_ref[...], wrlo_ref[...], br_ref[...], gate_ref, idx_ref)


def _mixer_prompt(x, nmix, win, convw, cos2, sin2, decay, xi, zeta, gl, rnorm, wout, nffn, wrhi, wrlo, br,
                  h_s, hn_s, idx_s, gate_s, cnt_s):
    n_l = SEQ // TILE_L
    tile = lambda s: jnp.minimum(s, N_TILES - 1)
    row = lambda s: (s, 0)
    const2 = lambda s: (0, 0)
    const3 = lambda s: (0, 0, 0)
    return pl.pallas_call(
        _mixer_prompt_kernel,
        grid=(N_TILES + 1,),
        in_specs=[
            pl.BlockSpec((1, TILE_L, D_MODEL), lambda s: (tile(s) // n_l, tile(s) % n_l, 0)),
            pl.BlockSpec((1, D_MODEL), const2),
            pl.BlockSpec((D_MODEL, PROJ_W), const2),
            pl.BlockSpec((CONV_K, CONV_W), const2),
            pl.BlockSpec((TILE_L, RET_HD), lambda s: (tile(s) % n_l, 0)),
            pl.BlockSpec((TILE_L, RET_HD), lambda s: (tile(s) % n_l, 0)),
            pl.BlockSpec((RET_HEADS, TILE_L, TILE_L), const3),
            pl.BlockSpec((RET_HEADS, TILE_L, RET_HD), const3),
            pl.BlockSpec((RET_HEADS, TILE_L, RET_HD), const3),
            pl.BlockSpec((RET_HEADS, 1, RET_HD), const3),
            pl.BlockSpec((1, RET_W), const2),
            pl.BlockSpec((D_MODEL, D_MODEL), const2),
            pl.BlockSpec((1, D_MODEL), const2),
            pl.BlockSpec((D_MODEL, N_EXPERTS), const2),
            pl.BlockSpec((D_MODEL, N_EXPERTS), const2),
            pl.BlockSpec((1, N_EXPERTS), const2),
            pl.BlockSpec((DEC_BATCH, D_MODEL), const2),
            pl.BlockSpec((DEC_BATCH * TOKEN_ROWS, LANES), const2),
            pl.BlockSpec((DEC_BATCH, SUBLANES), const2),
            pl.BlockSpec((DEC_BATCH, SUBLANES), const2),
            pl.BlockSpec((SUBLANES, N_EXPERTS), const2),
        ],
        out_specs=[
            pl.BlockSpec((TILE_L, D_MODEL), row),
            pl.BlockSpec((TILE_L * TOKEN_ROWS, LANES), row),
            pl.BlockSpec((TILE_L, SUBLANES), row),
            pl.BlockSpec((TILE_L, SUBLANES), row),
            pl.BlockSpec((SUBLANES, N_EXPERTS), const2),
            pl.BlockSpec((1, 1, CONV_K - 1, CONV_W), lambda s: (0, tile(s) // n_l, 0, 0)),
            pl.BlockSpec((1, 1, RET_HEADS, RET_HD, RET_HD), lambda s: (0, tile(s) // n_l, 0, 0, 0)),
        ],
        out_shape=[
            jax.ShapeDtypeStruct((T_BUF, D_MODEL), F32),
            jax.ShapeDtypeStruct((T_BUF * TOKEN_ROWS, LANES), F32),
            jax.ShapeDtypeStruct((T_BUF, SUBLANES), I32),
            jax.ShapeDtypeStruct((T_BUF, SUBLANES), F32),
            jax.ShapeDtypeStruct((SUBLANES, N_EXPERTS), F32),
            jax.ShapeDtypeStruct((1, BATCH, CONV_K - 1, CONV_W), F32),
            jax.ShapeDtypeStruct((1, BATCH, RET_HEADS, RET_HD, RET_HD), F32),
        ],
        scratch_shapes=[
            pltpu.VMEM((TILE_L + SUBLANES, CONV_W), F32),
            pltpu.VMEM((RET_HEADS, RET_HD, RET_HD), F32),
            pltpu.VMEM((TILE_L, D_MODEL), BF16),
        ],
        compiler_params=pltpu.CompilerParams(
            dimension_semantics=("arbitrary",), vmem_limit_bytes=VMEM_LIMIT),
        name="mixer_prompt",
    )(x, nmix, win, convw, cos2, sin2, decay, xi, zeta, gl, rnorm, wout, nffn, wrhi, wrlo, br,
      h_s, hn_s, idx_s, gate_s, cnt_s)


SAMPLE_GROUP = 8
N_GROUPS = DEC_BATCH // SAMPLE_GROUP


def _mixer_sample_kernel(x_ref, sconv_ref, sret_ref, nmix_ref, win_ref, convw_ref, cos_ref, sin_ref,
                         gamma_ref, rnorm_ref, wout_ref, nffn_ref, wrhi_ref, wrlo_ref, br_ref,
                         h_ref, hn_ref, idx_ref, gate_ref, cnt_ref, nconv_ref, nret_ref,
                         qt_buf, kt_buf, v_buf, o_buf, g_buf, mixbuf):
    grp = pl.program_id(0)

    @pl.when(grp == 0)
    def _():
        x = x_ref[...]
        xn = _rmsnorm(x, nmix_ref[...]).astype(BF16)
        pc = _dot(xn, win_ref[:, 0:3 * CONV_W])
        u = pc[:, 2 * CONV_W:3 * CONV_W] * pc[:, 0:CONV_W]
        st = sconv_ref[...]
        conv = (convw_ref[0:1, :] * st[:, 0:CONV_W] + convw_ref[1:2, :] * st[:, CONV_W:2 * CONV_W]
                + convw_ref[2:3, :] * u)
        mixbuf[:, 0:CONV_W] = (pc[:, CONV_W:2 * CONV_W] * conv).astype(BF16)
        nconv_ref[:, 0:CONV_W] = st[:, CONV_W:2 * CONV_W]
        nconv_ref[:, CONV_W:2 * CONV_W] = u

        pr = _dot(xn, win_ref[:, 3 * CONV_W:PROJ_W])
        g_buf[...] = pr[:, 3 * RET_W:4 * RET_W]
        cos2 = cos_ref[...]
        sin2 = sin_ref[...]
        for hh in range(RET_HEADS):
            lo = hh * RET_HD
            qr = _rotary(pr[:, lo:lo + RET_HD], cos2, sin2)
            kr = _rotary(pr[:, RET_W + lo:RET_W + lo + RET_HD], cos2, sin2) * (RET_HD ** -0.5)
            v = pr[:, 2 * RET_W + lo:2 * RET_W + lo + RET_HD]
            v_buf[hh] = v
            o_buf[hh] = jnp.sum(qr * kr, axis=-1, keepdims=True) * v
            qt = qr.T
            kt = kr.T
            for gg in range(N_GROUPS):
                qt_buf[hh, gg] = qt[:, gg * SAMPLE_GROUP:(gg + 1) * SAMPLE_GROUP]
                kt_buf[hh, gg] = kt[:, gg * SAMPLE_GROUP:(gg + 1) * SAMPLE_GROUP]

    base = pl.multiple_of(grp * SAMPLE_GROUP, SAMPLE_GROUP)
    for hh in range(RET_HEADS):
        gam = gamma_ref[hh]
        for j in range(SAMPLE_GROUP):
            state = sret_ref[j, hh]
            qc = qt_buf[hh, grp, :, j:j + 1]
            kc = kt_buf[hh, grp, :, j:j + 1]
            vrow = v_buf[hh, pl.ds(base + j, 1), :]
            cross = jnp.sum((qc * gam) * state, axis=0, keepdims=True)
            o_buf[hh, pl.ds(base + j, 1), :] = o_buf[hh, pl.ds(base + j, 1), :] + cross
            nret_ref[j, hh] = gam * state + kc * vrow

    @pl.when(grp == N_GROUPS - 1)
    def _():
        for hh in range(RET_HEADS):
            lo = hh * RET_HD
            r = _group_norm_gate(o_buf[hh], g_buf[:, lo:lo + RET_HD], rnorm_ref[:, lo:lo + RET_HD])
            mixbuf[:, CONV_W + lo:CONV_W + lo + RET_HD] = r.astype(BF16)
        h = x_ref[...] + _dot(mixbuf[...], wout_ref[...])
        h_ref[...] = h
        hn = _rmsnorm(h, nffn_ref[...])
        _store_token_tiles(hn_ref, hn)
        cnt_ref[...] = _route(hn, wrhi_ref[...], wrlo_ref[...], br_ref[...], gate_ref, idx_ref)


def _mixer_sample(x, sconv, sret, nmix, win, convw, cos2, sin2, gamma, rnorm, wout, nffn, wrhi, wrlo, br):
    const2 = lambda g: (0, 0)
    const3 = lambda g: (0, 0, 0)
    return pl.pallas_call(
        _mixer_sample_kernel,
        grid=(N_GROUPS,),
        in_specs=[
            pl.BlockSpec((DEC_BATCH, D_MODEL), const2),
            pl.BlockSpec((DEC_BATCH, (CONV_K - 1) * CONV_W), const2),
            pl.BlockSpec((SAMPLE_GROUP, RET_HEADS, RET_HD, RET_HD), lambda g: (g, 0, 0, 0)),
            pl.BlockSpec((1, D_MODEL), const2),
            pl.BlockSpec((D_MODEL, PROJ_W), const2),
            pl.BlockSpec((CONV_K, CONV_W), const2),
            pl.BlockSpec((1, RET_HD), const2),
            pl.BlockSpec((1, RET_HD), const2),
            pl.BlockSpec((RET_HEADS, 1, RET_HD), const3),
            pl.BlockSpec((1, RET_W), const2),
            pl.BlockSpec((D_MODEL, D_MODEL), const2),
            pl.BlockSpec((1, D_MODEL), const2),
            pl.BlockSpec((D_MODEL, N_EXPERTS), const2),
            pl.BlockSpec((D_MODEL, N_EXPERTS), const2),
            pl.BlockSpec((1, N_EXPERTS), const2),
        ],
        out_specs=[
            pl.BlockSpec((DEC_BATCH, D_MODEL), const2),
            pl.BlockSpec((DEC_BATCH * TOKEN_ROWS, LANES), const2),
            pl.BlockSpec((DEC_BATCH, SUBLANES), const2),
            pl.BlockSpec((DEC_BATCH, SUBLANES), const2),
            pl.BlockSpec((SUBLANES, N_EXPERTS), const2),
            pl.BlockSpec((DEC_BATCH, (CONV_K - 1) * CONV_W), const2),
            pl.BlockSpec((SAMPLE_GROUP, RET_HEADS, RET_HD, RET_HD), lambda g: (g, 0, 0, 0)),
        ],
        out_shape=[
            jax.ShapeDtypeStruct((DEC_BATCH, D_MODEL), F32),
            jax.ShapeDtypeStruct((DEC_BATCH * TOKEN_ROWS, LANES), F32),
            jax.ShapeDtypeStruct((DEC_BATCH, SUBLANES), I32),
            jax.ShapeDtypeStruct((DEC_BATCH, SUBLANES), F32),
            jax.ShapeDtypeStruct((SUBLANES, N_EXPERTS), F32),
            jax.ShapeDtypeStruct((DEC_BATCH, (CONV_K - 1) * CONV_W), F32),
            jax.ShapeDtypeStruct((DEC_BATCH, RET_HEADS, RET_HD, RET_HD), F32),
        ],
        scratch_shapes=[
            pltpu.VMEM((RET_HEADS, N_GROUPS, RET_HD, SAMPLE_GROUP), F32),
            pltpu.VMEM((RET_HEADS, N_GROUPS, RET_HD, SAMPLE_GROUP), F32),
            pltpu.VMEM((RET_HEADS, DEC_BATCH, RET_HD), F32),
            pltpu.VMEM((RET_HEADS, DEC_BATCH, RET_HD), F32),
            pltpu.VMEM((DEC_BATCH, RET_W), F32),
            pltpu.VMEM((DEC_BATCH, D_MODEL), BF16),
        ],
        compiler_params=pltpu.CompilerParams(
            dimension_semantics=("arbitrary",), vmem_limit_bytes=VMEM_LIMIT),
        name="mixer_sample",
    )(x, sconv, sret, nmix, win, convw, cos2, sin2, gamma, rnorm, wout, nffn, wrhi, wrlo, br)


def _route_rank_kernel(idx_ref, cnt_ref, dest_ref, be_ref, nb_ref, pstart_ref):
    cnt = cnt_ref[...]
    padded = ((cnt + (BLOCK_M - 1)) >> BLOCK_SHIFT) << BLOCK_SHIFT
    sub = lax.broadcasted_iota(I32, (N_EXPERTS, LANES), 0)
    pstart = jnp.zeros((N_EXPERTS, LANES), I32)
    for e in range(N_EXPERTS - 1):
        pstart = pstart + jnp.where(sub > e, padded[e:e + 1, :], 0)
    pend = pstart + padded
    pstart_ref[...] = pstart
    nb = pend[N_EXPERTS - 1:N_EXPERTS, :] >> BLOCK_SHIFT
    nb_ref[...] = nb

    blk = lax.broadcasted_iota(I32, (N_EXPERTS, BLOCK_LANES), 1)
    pend_w = jnp.concatenate([pend] * (BLOCK_LANES // LANES), axis=1)
    be = jnp.sum((pend_w <= blk * BLOCK_M).astype(I32), axis=0, keepdims=True)
    be = jnp.minimum(be, N_EXPERTS - 1)
    used = blk[0:1, :] < jnp.concatenate([nb] * (BLOCK_LANES // LANES), axis=1)
    last = jnp.max(jnp.where(used, be, 0), axis=1, keepdims=True)
    be_ref[...] = jnp.where(used, be, last)

    dest_ref[...] = jnp.zeros_like(dest_ref)
    upper = (lax.broadcasted_iota(I32, (LANES, LANES), 0)
             < lax.broadcasted_iota(I32, (LANES, LANES), 1)).astype(BF16)
    pstart_f = pstart.astype(F32)

    carry = jnp.zeros((N_EXPERTS, LANES), F32)
    for k in range(TOP_K):
        def chunk(c, carry, k=k):
            ids = idx_ref[k:k + 1, pl.ds(pl.multiple_of(c * LANES, LANES), LANES)]
            hit = sub == ids
            hit_f = hit.astype(F32)
            before = _dot(hit_f.astype(BF16), upper)
            pos = jnp.sum(jnp.where(hit, before + carry + pstart_f, 0.0), axis=0, keepdims=True)
            dest_ref[pl.ds(k * N_CHUNKS + c, 1), :] = pos.astype(I32)
            return carry + jnp.sum(hit_f, axis=1, keepdims=True)

        carry = lax.fori_loop(0, N_CHUNKS, chunk, carry)


def _route_rank(idx_lanes, cnt_col):
    return pl.pallas_call(
        _route_rank_kernel,
        out_shape=[
            jax.ShapeDtypeStruct((DEST_ROWS, LANES), I32),
            jax.ShapeDtypeStruct((1, BLOCK_LANES), I32),
            jax.ShapeDtypeStruct((1, LANES), I32),
            jax.ShapeDtypeStruct((N_EXPERTS, LANES), I32),
        ],
        name="route_rank",
    )(idx_lanes, cnt_col)


INVERT_ROWS = DEST_ROWS // 5
INVERT_GROUP = 16


def _pack_row(src_token, dst_row):
    return (jnp.asarray(dst_row, I32) << SRC_BITS) | jnp.asarray(src_token, I32)


def _pad_entry(p):
    return _pack_row(T_ALL + (p & (LANES - 1)), (p & (TOP_K - 1)) * SLAB + T_ALL + (p >> 2))


def _route_invert_kernel(cnt_ref, pstart_ref, dest_hbm, inv_ref, dest_s, sem):
    def pad_expert(e, _):
        n = cnt_ref[e]
        first = pstart_ref[e] + n
        n_pad = (((n + (BLOCK_M - 1)) >> BLOCK_SHIFT) << BLOCK_SHIFT) - n

        def pad_row(r, _):
            inv_ref[BLOCK_M + first + r] = _pad_entry(e * BLOCK_M + r)
            return 0

        lax.fori_loop(0, n_pad, pad_row, 0)
        return 0

    part = pl.program_id(0)

    @pl.when(part == 0)
    def _():
        lax.fori_loop(0, N_EXPERTS, pad_expert, 0)

        def lead_row(r, _):
            inv_ref[r] = _pad_entry(N_EXPERTS * BLOCK_M + r)
            return 0

        lax.fori_loop(0, BLOCK_M, lead_row, 0)

    row0 = part * INVERT_ROWS
    copy = pltpu.make_async_copy(
        dest_hbm.at[pl.ds(pl.multiple_of(row0 * LANES, LANES), INVERT_ROWS * LANES)], dest_s, sem)
    copy.start()
    copy.wait()

    def chunk(rr, _):
        row = row0 + rr
        k = row // N_CHUNKS
        tok0 = (row - k * N_CHUNKS) * LANES
        val0 = _pack_row(tok0, k * SLAB + tok0)
        base = rr * LANES
        for g in range(0, LANES, INVERT_GROUP):
            dests = [dest_s[base + g + i] for i in range(INVERT_GROUP)]
            for i, d in enumerate(dests):
                inv_ref[BLOCK_M + d] = val0 + (g + i) * _pack_row(1, 1)
        return 0

    lax.fori_loop(0, jnp.minimum(INVERT_ROWS, TOP_K * N_CHUNKS - row0), chunk, 0)


def _route_invert(dest, cnt, pstart):
    smem = pl.BlockSpec(memory_space=pltpu.SMEM)
    return pl.pallas_call(
        _route_invert_kernel,
        grid=(DEST_ROWS // INVERT_ROWS,),
        in_specs=[smem, smem, pl.BlockSpec(memory_space=pl.ANY)],
        out_specs=smem,
        out_shape=jax.ShapeDtypeStruct(((N_BLOCKS + 1) * BLOCK_M,), I32),
        scratch_shapes=[pltpu.SMEM((INVERT_ROWS * LANES,), I32), pltpu.SemaphoreType.DMA(())],
        compiler_params=pltpu.CompilerParams(dimension_semantics=("arbitrary",)),
        name="route_invert",
    )(cnt, pstart, dest)


X_SLOTS = 4


def _experts_kernel(inv_ref, be_ref, nb_ref, hn_hbm, wup_hbm, bup_hbm, wdn_hbm, bdn_hbm, y_hbm,
                    xbuf, obuf, actbuf, wup_f, wdn_f, wup_b, wdn_b, bup_v, bdn_v, sem_in, sem_out, sem_w):
    j = pl.program_id(0)
    nb = nb_ref[0]

    def token_rows(t):
        if isinstance(t, int):
            return pl.ds(t * TOKEN_ROWS, TOKEN_ROWS)
        return pl.ds(pl.multiple_of(t * TOKEN_ROWS, TOKEN_ROWS), TOKEN_ROWS)

    def gather_rows(blk, s):
        base = (blk + 1) * BLOCK_M
        for r in range(BLOCK_M):
            tok = inv_ref[base + r] & ((1 << SRC_BITS) - 1)
            pltpu.make_async_copy(hn_hbm.at[token_rows(tok)], xbuf.at[s, token_rows(r)], sem_in.at[s]).start()

    def scatter_rows(blk, s):
        base = (blk + 1) * BLOCK_M
        for r in range(BLOCK_M):
            row = lax.shift_right_logical(inv_ref[base + r], SRC_BITS)
            pltpu.make_async_copy(obuf.at[s, token_rows(r)], y_hbm.at[token_rows(row)], sem_out.at[s]).start()

    def wait_gather(s):
        pltpu.make_async_copy(hn_hbm.at[pl.ds(0, BLOCK_M * TOKEN_ROWS)], xbuf.at[s], sem_in.at[s]).wait()

    def wait_scatter(s):
        pltpu.make_async_copy(obuf.at[s], y_hbm.at[pl.ds(0, BLOCK_M * TOKEN_ROWS)], sem_out.at[s]).wait()

    @pl.when(j == 0)
    def _():
        gather_rows(0, 0)
        gather_rows(jnp.minimum(1, nb - 1), 1)
        obuf[1] = jnp.zeros(obuf.shape[1:], F32)

    def weight_copies(e):
        return (pltpu.make_async_copy(wup_hbm.at[e], wup_f, sem_w.at[0]),
                pltpu.make_async_copy(wdn_hbm.at[e], wdn_f, sem_w.at[1]))

    def bias_copies():
        return (pltpu.make_async_copy(bup_hbm, bup_v, sem_w.at[2]),
                pltpu.make_async_copy(bdn_hbm, bdn_v, sem_w.at[3]))

    expert = be_ref[j]

    @pl.when(j == 0)
    def _():
        for c in weight_copies(expert) + bias_copies():
            c.start()

    @pl.when((j < nb) & ((j == 0) | (expert != be_ref[jnp.maximum(j - 1, 0)])))
    def _():
        for c in weight_copies(expert):
            c.wait()

        @pl.when(j == 0)
        def _():
            for c in bias_copies():
                c.wait()

        wup_b[...] = wup_f[...].astype(BF16)
        wdn_b[...] = wdn_f[...].astype(BF16)
        nxt = lax.while_loop(
            lambda jj: (jj < nb) & (be_ref[jnp.minimum(jj, N_BLOCKS - 1)] == expert), lambda jj: jj + 1, j + 1)

        @pl.when(nxt < nb)
        def _():
            for c in weight_copies(be_ref[jnp.minimum(nxt, N_BLOCKS - 1)]):
                c.start()

    def step(x_slot):
        o_slot = x_slot % 2

        wait_gather(x_slot)
        scatter_rows(j - 1, 1 - o_slot)
        x = _load_token_tiles(xbuf.at[x_slot]).astype(BF16)
        hmid = _dot(x, wup_b[...]) + bup_v[expert]
        h_glu = jnp.minimum(hmid[:, 0:D_FF], SWIGLU_LIMIT)
        h_lin = jnp.clip(hmid[:, D_FF:2 * D_FF], -SWIGLU_LIMIT, SWIGLU_LIMIT)
        actbuf[...] = (h_glu * jax.nn.sigmoid(SWIGLU_ALPHA * h_glu) * (h_lin + 1.0)).astype(BF16)

        @pl.when(j >= 1)
        def _():
            wait_scatter(o_slot)

        gather_rows(jnp.minimum(j + 2, nb - 1), (x_slot + 2) % X_SLOTS)
        _store_token_tiles(obuf.at[o_slot], _dot(actbuf[...], wdn_b[...]) + bdn_v[expert])

        @pl.when(j == nb - 1)
        def _():
            scatter_rows(j, o_slot)
            wait_scatter(1 - o_slot)
            wait_scatter(o_slot)
            wait_gather((x_slot + 1) % X_SLOTS)
            wait_gather((x_slot + 2) % X_SLOTS)

    for x_slot in range(X_SLOTS):
        pl.when((j < nb) & (j % X_SLOTS == x_slot))(functools.partial(step, x_slot))


def _experts(inv, be, nb, hn_all, w_up, b_up, w_down, b_down):
    grid_spec = pltpu.PrefetchScalarGridSpec(
        num_scalar_prefetch=3,
        grid=(N_BLOCKS,),
        in_specs=[pl.BlockSpec(memory_space=pl.ANY)] * 5,
        out_specs=pl.BlockSpec(memory_space=pl.ANY),
        scratch_shapes=[
            pltpu.VMEM((X_SLOTS, BLOCK_M * TOKEN_ROWS, LANES), F32),
            pltpu.VMEM((2, BLOCK_M * TOKEN_ROWS, LANES), F32),
            pltpu.VMEM((BLOCK_M, D_FF), BF16),
            pltpu.VMEM((D_MODEL, 2 * D_FF), F32),
            pltpu.VMEM((D_FF, D_MODEL), F32),
            pltpu.VMEM((D_MODEL, 2 * D_FF), BF16),
            pltpu.VMEM((D_FF, D_MODEL), BF16),
            pltpu.VMEM((N_EXPERTS, 1, 2 * D_FF), F32),
            pltpu.VMEM((N_EXPERTS, 1, D_MODEL), F32),
            pltpu.SemaphoreType.DMA((X_SLOTS,)),
            pltpu.SemaphoreType.DMA((2,)),
            pltpu.SemaphoreType.DMA((4,)),
        ],
    )
    return pl.pallas_call(
        _experts_kernel,
        grid_spec=grid_spec,
        out_shape=jax.ShapeDtypeStruct((TOP_K * SLAB * TOKEN_ROWS, LANES), F32),
        compiler_params=pltpu.CompilerParams(
            dimension_semantics=("arbitrary",), vmem_limit_bytes=VMEM_LIMIT),
        name="experts",
    )(inv, be, nb, hn_all, w_up, b_up, w_down, b_down)


def _combine_kernel(ys_ref, gate_ref, h_ref, nfin_ref, out_ref):
    gates = gate_ref[...]
    acc = h_ref[...]
    for k in range(TOP_K):
        acc = acc + gates[:, k:k + 1] * _load_token_tiles(ys_ref.at[k])
    out_ref[...] = _rmsnorm(acc, nfin_ref[...])


def _combine(ys, gates, h_all, nfin, rows, tile, first_block, name):
    return pl.pallas_call(
        _combine_kernel,
        grid=(rows // tile,),
        in_specs=[
            pl.BlockSpec((TOP_K, tile * TOKEN_ROWS, LANES), lambda i: (0, first_block + i, 0)),
            pl.BlockSpec((tile, SUBLANES), lambda i: (first_block + i, 0)),
            pl.BlockSpec((tile, D_MODEL), lambda i: (first_block + i, 0)),
            pl.BlockSpec((1, D_MODEL), lambda i: (0, 0)),
        ],
        out_specs=pl.BlockSpec((tile, D_MODEL), lambda i: (i, 0)),
        out_shape=jax.ShapeDtypeStruct((rows, D_MODEL), F32),
        compiler_params=pltpu.CompilerParams(
            dimension_semantics=("arbitrary",), vmem_limit_bytes=VMEM_LIMIT),
        name=name,
    )(ys, gates, h_all, nfin)


def _rope_tables(positions):
    half = RET_HD // 2
    inv = ROPE_BASE ** (-jnp.arange(half, dtype=F32) / half)
    ang = positions.astype(F32)[:, None] * inv[None, :]
    cos, sin = jnp.cos(ang), jnp.sin(ang)
    return jnp.concatenate([cos, cos], axis=-1), jnp.concatenate([-sin, sin], axis=-1)


def _decay_tables(log_gamma, c):
    pos = jnp.arange(c, dtype=F32)
    diff = pos[:, None] - pos[None, :]
    causal = diff >= 0
    lg = log_gamma[:, None, None]
    decay = jnp.where(causal, jnp.exp(lg * jnp.where(causal, diff, 0.0)), 0.0)
    xi = jnp.exp(log_gamma[:, None] * (pos[None, :] + 1.0))
    zeta = jnp.exp(log_gamma[:, None] * (c - 1.0 - pos[None, :]))
    wide = lambda t: jnp.broadcast_to(t[..., None], t.shape + (RET_HD,))
    gl = jnp.broadcast_to(jnp.exp(log_gamma * c)[:, None, None], (RET_HEADS, 1, RET_HD))
    return decay, wide(xi), wide(zeta), gl


def kernel(x_prompt, x_sample, state_conv, state_ret, norm_mix, w_in, conv_w, ret_norm, w_out, norm_ffn,
           w_router, b_router, w_up, b_up, w_down, b_down, norm_final):
    assert norm_mix.shape[0] == 1, "single trunk layer"
    nmix = norm_mix[0][None, :]
    nffn = norm_ffn[0][None, :]
    nfin = norm_final[None, :]
    rnorm = ret_norm[0][None, :]
    win = w_in[0].astype(BF16)
    wout = w_out[0].astype(BF16)
    wr = w_router[0]
    wrhi = wr.astype(BF16)
    wrlo = (wr - wrhi.astype(F32)).astype(BF16)
    br = b_router[0][None, :]
    convw = conv_w[0]

    log_gamma = jnp.log(1.0 - 2.0 ** (-5.0 - jnp.arange(RET_HEADS, dtype=F32)))
    decay, xi, zeta, gl = _decay_tables(log_gamma, TILE_L)
    cos_p, sin_p = _rope_tables(jnp.arange(SEQ, dtype=jnp.int32))
    cos_s, sin_s = _rope_tables(PAST_LEN + jnp.arange(1, dtype=jnp.int32))
    gamma1 = jnp.broadcast_to(jnp.exp(log_gamma)[:, None, None], (RET_HEADS, 1, RET_HD))

    h_s, hn_s, idx_s, gate_s, cnt_s, conv_s, ret_s = _mixer_sample(
        x_sample.reshape(DEC_BATCH, D_MODEL),
        state_conv[0].reshape(DEC_BATCH, (CONV_K - 1) * CONV_W),
        state_ret[0], nmix, win, convw, cos_s, sin_s, gamma1, rnorm, wout, nffn, wrhi, wrlo, br)

    h_all, hn_all, idx_all, gate_all, cnt, conv_p, ret_p = _mixer_prompt(
        x_prompt, nmix, win, convw, cos_p, sin_p, decay, xi, zeta, gl, rnorm, wout, nffn, wrhi, wrlo, br,
        h_s, hn_s, idx_s, gate_s, cnt_s)

    idx_lanes = idx_all[0:T_ALL, 0:TOP_K].T
    cnt_i = cnt[0].astype(I32)
    cnt_col = jnp.broadcast_to(cnt_i[:, None], (N_EXPERTS, LANES))
    dest, be, nb, pstart_col = _route_rank(idx_lanes, cnt_col)
    inv = _route_invert(dest.reshape(DEST_ROWS * LANES), cnt_i, pstart_col[:, 0])

    ys = _experts(inv, be[0, 0:N_BLOCKS], nb[0, 0:1], hn_all,
                  w_up[0], b_up[0][:, None, :], w_down[0], b_down[0][:, None, :])
    ys = ys.reshape(TOP_K, SLAB * TOKEN_ROWS, LANES)

    y_prompt = _combine(ys, gate_all, h_all, nfin, T_PROMPT, FINAL_TILE, 0, "combine_prompt")
    y_sample = _combine(ys, gate_all, h_all, nfin, DEC_BATCH, DEC_BATCH, T_PROMPT // DEC_BATCH,
                        "combine_sample")

    return (y_prompt.reshape(BATCH, SEQ, D_MODEL),
            y_sample.reshape(DEC_BATCH, 1, D_MODEL),
            conv_p,
            ret_p,
            conv_s.reshape(1, DEC_BATCH, CONV_K - 1, CONV_W),
            ret_s.reshape(1, DEC_BATCH, RET_HEADS, RET_HD, RET_HD))
```

```python
import functools

import jax
import jax.numpy as jnp
from jax import lax
from jax.experimental import pallas as pl
from jax.experimental.pallas import tpu as pltpu

F32 = jnp.float32
BF16 = jnp.bfloat16
I32 = jnp.int32

D_MODEL = 1024
BATCH = 8
SEQ = 2048
DEC_BATCH = 128
PAST_LEN = 16384
CONV_W = 512
CONV_K = 3
RET_W = 512
RET_HEADS = 4
RET_HD = 128
ROPE_BASE = 10000.0
PROJ_W = 3 * CONV_W + 4 * RET_W
N_EXPERTS = 32
TOP_K = 4
D_FF = D_MODEL
SWIGLU_LIMIT = 7.0
SWIGLU_ALPHA = 1.702
RMS_EPS = 1e-6
GN_EPS = 1e-5

T_PROMPT = BATCH * SEQ
T_ALL = T_PROMPT + DEC_BATCH
N_ASSIGN = T_ALL * TOP_K

LANES = 128
SUBLANES = 8
TOKEN_ROWS = D_MODEL // LANES
TILE_L = 256
N_TILES = T_PROMPT // TILE_L
T_BUF = (N_TILES + 1) * TILE_L
BLOCK_M = 256
BLOCK_SHIFT = 8
N_BLOCKS = N_ASSIGN // BLOCK_M + N_EXPERTS
BLOCK_LANES = 3 * LANES
DUMP_ROWS = (N_EXPERTS + 1) * BLOCK_M // TOP_K
SLAB = T_ALL + DUMP_ROWS
SRC_BITS = 15
N_CHUNKS = T_ALL // LANES
DEST_ROWS = 520
FINAL_TILE = 512
VMEM_LIMIT = 56 * 1024 * 1024

assert N_ASSIGN % BLOCK_M == 0 and N_BLOCKS <= BLOCK_LANES and (1 << BLOCK_SHIFT) == BLOCK_M
assert T_ALL % LANES == 0 and TOP_K * N_CHUNKS <= DEST_ROWS
assert T_BUF <= (1 << SRC_BITS) and TOP_K * SLAB <= (1 << (32 - SRC_BITS)) and T_ALL + LANES <= T_BUF


def _rmsnorm(x, g):
    ms = jnp.mean(x * x, axis=-1, keepdims=True)
    return x * lax.rsqrt(ms + RMS_EPS) * g


def _store_token_tiles(ref, val):
    rows = val.shape[0]
    for s in range(TOKEN_ROWS):
        ref[pl.ds(s, rows, stride=TOKEN_ROWS), :] = val[:, s * LANES:(s + 1) * LANES]


def _load_token_tiles(ref):
    rows = ref.shape[0] // TOKEN_ROWS
    return jnp.concatenate([ref[pl.ds(s, rows, stride=TOKEN_ROWS), :] for s in range(TOKEN_ROWS)], axis=1)


def _dot(a, b):
    return jnp.dot(a, b, preferred_element_type=F32)


def _dot_nt(a, b):
    return lax.dot_general(a, b, (((1,), (1,)), ((), ())), preferred_element_type=F32)


def _dot_tn(a, b):
    return lax.dot_general(a, b, (((0,), (0,)), ((), ())), preferred_element_type=F32)


def _rotary(x, cos2, sin2):
    return x * cos2 + pltpu.roll(x, RET_HD // 2, 1) * sin2


def _group_norm_gate(o, g, w):
    mu = jnp.mean(o, axis=-1, keepdims=True)
    d = o - mu
    var = jnp.mean(d * d, axis=-1, keepdims=True)
    return (g * jax.nn.sigmoid(g)) * (d * lax.rsqrt(var + GN_EPS) * w)


def _route(hn, wr_hi, wr_lo, b_router, gate_ref, idx_ref):
    rows = hn.shape[0]
    hi = hn.astype(BF16)
    lo = (hn - hi.astype(F32)).astype(BF16)
    logits = _dot(hi, wr_hi) + _dot(lo, wr_hi) + _dot(hi, wr_lo) + b_router
    lane = lax.broadcasted_iota(I32, (rows, N_EXPERTS), 1)
    vals, sels = [], []
    onehot = jnp.zeros((rows, N_EXPERTS), F32)
    for _ in range(TOP_K):
        m = jnp.max(logits, axis=-1, keepdims=True)
        sel = jnp.min(jnp.where(logits == m, lane, N_EXPERTS), axis=-1, keepdims=True)
        hit = lane == sel
        onehot = onehot + hit.astype(F32)
        logits = jnp.where(hit, -jnp.inf, logits)
        vals.append(m)
        sels.append(sel)
    ex = [jnp.exp(v - vals[0]) for v in vals]
    den = ex[0] + ex[1] + ex[2] + ex[3]
    zero = jnp.zeros((rows, SUBLANES - TOP_K), F32)
    gate_ref[...] = jnp.concatenate([e / den for e in ex] + [zero], axis=1)
    idx_ref[...] = jnp.concatenate(sels + [zero.astype(I32)], axis=1)
    return jnp.sum(onehot.reshape(rows // SUBLANES, SUBLANES, N_EXPERTS), axis=0)


def _mixer_prompt_kernel(x_ref, nmix_ref, win_ref, convw_ref, cos_ref, sin_ref, decay_ref, xi_ref,
                         zeta_ref, gl_ref, rnorm_ref, wout_ref, nffn_ref, wrhi_ref, wrlo_ref, br_ref,
                         hs_ref, hns_ref, idxs_ref, gates_ref, cnts_ref,
                         h_ref, hn_ref, idx_ref, gate_ref, cnt_ref, nconv_ref, nret_ref,
                         ubuf, sbuf, mixbuf):
    step = pl.program_id(0)

    @pl.when(step == 0)
    def _():
        cnt_ref[...] = cnts_ref[...]

    @pl.when(step < N_TILES)
    def _():
        _mixer_prompt_tile(step % (SEQ // TILE_L), x_ref, nmix_ref, win_ref, convw_ref, cos_ref, sin_ref,
                           decay_ref, xi_ref, zeta_ref, gl_ref, rnorm_ref, wout_ref, nffn_ref, wrhi_ref,
                           wrlo_ref, br_ref, h_ref, hn_ref, idx_ref, gate_ref, cnt_ref, nconv_ref, nret_ref,
                           ubuf, sbuf, mixbuf)

    @pl.when(step == N_TILES)
    def _():
        for dst, src in ((h_ref, hs_ref), (hn_ref, hns_ref), (idx_ref, idxs_ref), (gate_ref, gates_ref)):
            n_src, n_dst = src.shape[0], dst.shape[0]
            dst[0:n_src, :] = src[...]
            dst[n_src:n_dst, :] = jnp.zeros((n_dst - n_src, dst.shape[1]), dst.dtype)
        cnt = cnt_ref[...]
        cnt_ref[...] = jnp.broadcast_to(jnp.sum(cnt, axis=0, keepdims=True), cnt.shape)


def _mixer_prompt_tile(i, x_ref, nmix_ref, win_ref, convw_ref, cos_ref, sin_ref, decay_ref, xi_ref,
                       zeta_ref, gl_ref, rnorm_ref, wout_ref, nffn_ref, wrhi_ref, wrlo_ref, br_ref,
                       h_ref, hn_ref, idx_ref, gate_ref, cnt_ref, nconv_ref, nret_ref,
                       ubuf, sbuf, mixbuf):
    @pl.when(i == 0)
    def _():
        ubuf[0:SUBLANES, :] = jnp.zeros((SUBLANES, CONV_W), F32)
        sbuf[...] = jnp.zeros_like(sbuf)

    x = x_ref[0]
    xn = _rmsnorm(x, nmix_ref[...]).astype(BF16)

    pc = _dot(xn, win_ref[:, 0:3 * CONV_W])
    u = pc[:, 2 * CONV_W:3 * CONV_W] * pc[:, 0:CONV_W]
    ubuf[SUBLANES:SUBLANES + TILE_L, :] = u
    conv = (convw_ref[0:1, :] * ubuf[SUBLANES - 2:SUBLANES - 2 + TILE_L, :]
            + convw_ref[1:2, :] * ubuf[SUBLANES - 1:SUBLANES - 1 + TILE_L, :]
            + convw_ref[2:3, :] * u)
    mixbuf[:, 0:CONV_W] = (pc[:, CONV_W:2 * CONV_W] * conv).astype(BF16)
    nconv_ref[0, 0] = u[TILE_L - (CONV_K - 1):TILE_L, :]
    ubuf[0:SUBLANES, :] = u[TILE_L - SUBLANES:TILE_L, :]

    pr = _dot(xn, win_ref[:, 3 * CONV_W:PROJ_W])
    cos2 = cos_ref[...]
    sin2 = sin_ref[...]
    for hh in range(RET_HEADS):
        lo = hh * RET_HD
        q = pr[:, lo:lo + RET_HD]
        k = pr[:, RET_W + lo:RET_W + lo + RET_HD]
        v = pr[:, 2 * RET_W + lo:2 * RET_W + lo + RET_HD]
        g = pr[:, 3 * RET_W + lo:3 * RET_W + lo + RET_HD]
        qr = _rotary(q, cos2, sin2)
        kr = _rotary(k, cos2, sin2) * (RET_HD ** -0.5)
        vb = v.astype(BF16)
        scores = _dot_nt(qr.astype(BF16), kr.astype(BF16)) * decay_ref[hh]
        state = sbuf[hh]
        o = _dot(scores.astype(BF16), vb) + _dot((qr * xi_ref[hh]).astype(BF16), state.astype(BF16))
        sbuf[hh] = gl_ref[hh] * state + _dot_tn((kr * zeta_ref[hh]).astype(BF16), vb)
        r = _group_norm_gate(o, g, rnorm_ref[:, lo:lo + RET_HD])
        mixbuf[:, CONV_W + lo:CONV_W + lo + RET_HD] = r.astype(BF16)
    nret_ref[0, 0] = sbuf[...]

    h = x + _dot(mixbuf[...], wout_ref[...])
    h_ref[...] = h
    hn = _rmsnorm(h, nffn_ref[...])
    _store_token_tiles(hn_ref, hn)
    cnt_ref[...] += _route(hn, wrhi_ref[...], wrlo_ref[...], br_ref[...], gate_ref, idx_ref)


def _mixer_prompt(x, nmix, win, convw, cos2, sin2, decay, xi, zeta, gl, rnorm, wout, nffn, wrhi, wrlo, br,
                  h_s, hn_s, idx_s, gate_s, cnt_s):
    n_l = SEQ // TILE_L
    tile = lambda s: jnp.minimum(s, N_TILES - 1)
    row = lambda s: (s, 0)
    const2 = lambda s: (0, 0)
    const3 = lambda s: (0, 0, 0)
    return pl.pallas_call(
        _mixer_prompt_kernel,
        grid=(N_TILES + 1,),
        in_specs=[
            pl.BlockSpec((1, TILE_L, D_MODEL), lambda s: (tile(s) // n_l, tile(s) % n_l, 0)),
            pl.BlockSpec((1, D_MODEL), const2),
            pl.BlockSpec((D_MODEL, PROJ_W), const2),
            pl.BlockSpec((CONV_K, CONV_W), const2),
            pl.BlockSpec((TILE_L, RET_HD), lambda s: (tile(s) % n_l, 0)),
            pl.BlockSpec((TILE_L, RET_HD), lambda s: (tile(s) % n_l, 0)),
            pl.BlockSpec((RET_HEADS, TILE_L, TILE_L), const3),
            pl.BlockSpec((RET_HEADS, TILE_L, RET_HD), const3),
            pl.BlockSpec((RET_HEADS, TILE_L, RET_HD), const3),
            pl.BlockSpec((RET_HEADS, 1, RET_HD), const3),
            pl.BlockSpec((1, RET_W), const2),
            pl.BlockSpec((D_MODEL, D_MODEL), const2),
            pl.BlockSpec((1, D_MODEL), const2),
            pl.BlockSpec((D_MODEL, N_EXPERTS), const2),
            pl.BlockSpec((D_MODEL, N_EXPERTS), const2),
            pl.BlockSpec((1, N_EXPERTS), const2),
            pl.BlockSpec((DEC_BATCH, D_MODEL), const2),
            pl.BlockSpec((DEC_BATCH * TOKEN_ROWS, LANES), const2),
            pl.BlockSpec((DEC_BATCH, SUBLANES), const2),
            pl.BlockSpec((DEC_BATCH, SUBLANES), const2),
            pl.BlockSpec((SUBLANES, N_EXPERTS), const2),
        ],
        out_specs=[
            pl.BlockSpec((TILE_L, D_MODEL), row),
            pl.BlockSpec((TILE_L * TOKEN_ROWS, LANES), row),
            pl.BlockSpec((TILE_L, SUBLANES), row),
            pl.BlockSpec((TILE_L, SUBLANES), row),
            pl.BlockSpec((SUBLANES, N_EXPERTS), const2),
            pl.BlockSpec((1, 1, CONV_K - 1, CONV_W), lambda s: (0, tile(s) // n_l, 0, 0)),
            pl.BlockSpec((1, 1, RET_HEADS, RET_HD, RET_HD), lambda s: (0, tile(s) // n_l, 0, 0, 0)),
        ],
        out_shape=[
            jax.ShapeDtypeStruct((T_BUF, D_MODEL), F32),
            jax.ShapeDtypeStruct((T_BUF * TOKEN_ROWS, LANES), F32),
            jax.ShapeDtypeStruct((T_BUF, SUBLANES), I32),
            jax.ShapeDtypeStruct((T_BUF, SUBLANES), F32),
            jax.ShapeDtypeStruct((SUBLANES, N_EXPERTS), F32),
            jax.ShapeDtypeStruct((1, BATCH, CONV_K - 1, CONV_W), F32),
            jax.ShapeDtypeStruct((1, BATCH, RET_HEADS, RET_HD, RET_HD), F32),
        ],
        scratch_shapes=[
            pltpu.VMEM((TILE_L + SUBLANES, CONV_W), F32),
            pltpu.VMEM((RET_HEADS, RET_HD, RET_HD), F32),
            pltpu.VMEM((TILE_L, D_MODEL), BF16),
        ],
        compiler_params=pltpu.CompilerParams(
            dimension_semantics=("arbitrary",), vmem_limit_bytes=VMEM_LIMIT),
        name="mixer_prompt",
    )(x, nmix, win, convw, cos2, sin2, decay, xi, zeta, gl, rnorm, wout, nffn, wrhi, wrlo, br,
      h_s, hn_s, idx_s, gate_s, cnt_s)


SAMPLE_GROUP = 8
N_GROUPS = DEC_BATCH // SAMPLE_GROUP


def _mixer_sample_kernel(x_ref, sconv_ref, sret_ref, nmix_ref, win_ref, convw_ref, cos_ref, sin_ref,
                         gamma_ref, rnorm_ref, wout_ref, nffn_ref, wrhi_ref, wrlo_ref, br_ref,
                         h_ref, hn_ref, idx_ref, gate_ref, cnt_ref, nconv_ref, nret_ref,
                         qt_buf, kt_buf, v_buf, o_buf, g_buf, mixbuf):
    grp = pl.program_id(0)

    @pl.when(grp == 0)
    def _():
        x = x_ref[...]
        xn = _rmsnorm(x, nmix_ref[...]).astype(BF16)
        pc = _dot(xn, win_ref[:, 0:3 * CONV_W])
        u = pc[:, 2 * CONV_W:3 * CONV_W] * pc[:, 0:CONV_W]
        st = sconv_ref[...]
        conv = (convw_ref[0:1, :] * st[:, 0:CONV_W] + convw_ref[1:2, :] * st[:, CONV_W:2 * CONV_W]
                + convw_ref[2:3, :] * u)
        mixbuf[:, 0:CONV_W] = (pc[:, CONV_W:2 * CONV_W] * conv).astype(BF16)
        nconv_ref[:, 0:CONV_W] = st[:, CONV_W:2 * CONV_W]
        nconv_ref[:, CONV_W:2 * CONV_W] = u

        pr = _dot(xn, win_ref[:, 3 * CONV_W:PROJ_W])
        g_buf[...] = pr[:, 3 * RET_W:4 * RET_W]
        cos2 = cos_ref[...]
        sin2 = sin_ref[...]
        for hh in range(RET_HEADS):
            lo = hh * RET_HD
            qr = _rotary(pr[:, lo:lo + RET_HD], cos2, sin2)
            kr = _rotary(pr[:, RET_W + lo:RET_W + lo + RET_HD], cos2, sin2) * (RET_HD ** -0.5)
            v = pr[:, 2 * RET_W + lo:2 * RET_W + lo + RET_HD]
            v_buf[hh] = v
            o_buf[hh] = jnp.sum(qr * kr, axis=-1, keepdims=True) * v
            qt = qr.T
            kt = kr.T
            for gg in range(N_GROUPS):
                qt_buf[hh, gg] = qt[:, gg * SAMPLE_GROUP:(gg + 1) * SAMPLE_GROUP]
                kt_buf[hh, gg] = kt[:, gg * SAMPLE_GROUP:(gg + 1) * SAMPLE_GROUP]

    base = pl.multiple_of(grp * SAMPLE_GROUP, SAMPLE_GROUP)
    for hh in range(RET_HEADS):
        gam = gamma_ref[hh]
        for j in range(SAMPLE_GROUP):
            state = sret_ref[j, hh]
            qc = qt_buf[hh, grp, :, j:j + 1]
            kc = kt_buf[hh, grp, :, j:j + 1]
            vrow = v_buf[hh, pl.ds(base + j, 1), :]
            cross = jnp.sum((qc * gam) * state, axis=0, keepdims=True)
            o_buf[hh, pl.ds(base + j, 1), :] = o_buf[hh, pl.ds(base + j, 1), :] + cross
            nret_ref[j, hh] = gam * state + kc * vrow

    @pl.when(grp == N_GROUPS - 1)
    def _():
        for hh in range(RET_HEADS):
            lo = hh * RET_HD
            r = _group_norm_gate(o_buf[hh], g_buf[:, lo:lo + RET_HD], rnorm_ref[:, lo:lo + RET_HD])
            mixbuf[:, CONV_W + lo:CONV_W + lo + RET_HD] = r.astype(BF16)
        h = x_ref[...] + _dot(mixbuf[...], wout_ref[...])
        h_ref[...] = h
        hn = _rmsnorm(h, nffn_ref[...])
        _store_token_tiles(hn_ref, hn)
        cnt_ref[...] = _route(hn, wrhi_ref[...], wrlo_ref[...], br_ref[...], gate_ref, idx_ref)


def _mixer_sample(x, sconv, sret, nmix, win, convw, cos2, sin2, gamma, rnorm, wout, nffn, wrhi, wrlo, br):
    const2 = lambda g: (0, 0)
    const3 = lambda g: (0, 0, 0)
    return pl.pallas_call(
        _mixer_sample_kernel,
        grid=(N_GROUPS,),
        in_specs=[
            pl.BlockSpec((DEC_BATCH, D_MODEL), const2),
            pl.BlockSpec((DEC_BATCH, (CONV_K - 1) * CONV_W), const2),
            pl.BlockSpec((SAMPLE_GROUP, RET_HEADS, RET_HD, RET_HD), lambda g: (g, 0, 0, 0)),
            pl.BlockSpec((1, D_MODEL), const2),
            pl.BlockSpec((D_MODEL, PROJ_W), const2),
            pl.BlockSpec((CONV_K, CONV_W), const2),
            pl.BlockSpec((1, RET_HD), const2),
            pl.BlockSpec((1, RET_HD), const2),
            pl.BlockSpec((RET_HEADS, 1, RET_HD), const3),
            pl.BlockSpec((1, RET_W), const2),
            pl.BlockSpec((D_MODEL, D_MODEL), const2),
            pl.BlockSpec((1, D_MODEL), const2),
            pl.BlockSpec((D_MODEL, N_EXPERTS), const2),
            pl.BlockSpec((D_MODEL, N_EXPERTS), const2),
            pl.BlockSpec((1, N_EXPERTS), const2),
        ],
        out_specs=[
            pl.BlockSpec((DEC_BATCH, D_MODEL), const2),
            pl.BlockSpec((DEC_BATCH * TOKEN_ROWS, LANES), const2),
            pl.BlockSpec((DEC_BATCH, SUBLANES), const2),
            pl.BlockSpec((DEC_BATCH, SUBLANES), const2),
            pl.BlockSpec((SUBLANES, N_EXPERTS), const2),
            pl.BlockSpec((DEC_BATCH, (CONV_K - 1) * CONV_W), const2),
            pl.BlockSpec((SAMPLE_GROUP, RET_HEADS, RET_HD, RET_HD), lambda g: (g, 0, 0, 0)),
        ],
        out_shape=[
            jax.ShapeDtypeStruct((DEC_BATCH, D_MODEL), F32),
            jax.ShapeDtypeStruct((DEC_BATCH * TOKEN_ROWS, LANES), F32),
            jax.ShapeDtypeStruct((DEC_BATCH, SUBLANES), I32),
            jax.ShapeDtypeStruct((DEC_BATCH, SUBLANES), F32),
            jax.ShapeDtypeStruct((SUBLANES, N_EXPERTS), F32),
            jax.ShapeDtypeStruct((DEC_BATCH, (CONV_K - 1) * CONV_W), F32),
            jax.ShapeDtypeStruct((DEC_BATCH, RET_HEADS, RET_HD, RET_HD), F32),
        ],
        scratch_shapes=[
            pltpu.VMEM((RET_HEADS, N_GROUPS, RET_HD, SAMPLE_GROUP), F32),
            pltpu.VMEM((RET_HEADS, N_GROUPS, RET_HD, SAMPLE_GROUP), F32),
            pltpu.VMEM((RET_HEADS, DEC_BATCH, RET_HD), F32),
            pltpu.VMEM((RET_HEADS, DEC_BATCH, RET_HD), F32),
            pltpu.VMEM((DEC_BATCH, RET_W), F32),
            pltpu.VMEM((DEC_BATCH, D_MODEL), BF16),
        ],
        compiler_params=pltpu.CompilerParams(
            dimension_semantics=("arbitrary",), vmem_limit_bytes=VMEM_LIMIT),
        name="mixer_sample",
    )(x, sconv, sret, nmix, win, convw, cos2, sin2, gamma, rnorm, wout, nffn, wrhi, wrlo, br)


GROUP_ROWS = SUBLANES
N_GROUPS_RANK = DEST_ROWS // GROUP_ROWS
RANK_LAG = 2
INVERT_GROUP = 16


def _pack_row(src_token, dst_row):
    return (jnp.asarray(dst_row, I32) << SRC_BITS) | jnp.asarray(src_token, I32)


def _pad_entry(p):
    return _pack_row(T_ALL + (p & (LANES - 1)), (p & (TOP_K - 1)) * SLAB + T_ALL + (p >> 2))


def _route_tables_kernel(cnts_ref, idx_ref, cnt_ref, inv_ref, be_ref, nb_ref,
                         carry_v, pstart_v, stage_v, stage_s, sem):
    g = pl.program_id(0)
    sub = lax.broadcasted_iota(I32, (N_EXPERTS, LANES), 0)

    @pl.when(g == 0)
    def _():
        cnt = cnt_ref[...]
        padded = ((cnt + (BLOCK_M - 1)) >> BLOCK_SHIFT) << BLOCK_SHIFT
        pstart = jnp.zeros((N_EXPERTS, LANES), I32)
        for e in range(N_EXPERTS - 1):
            pstart = pstart + jnp.where(sub > e, padded[e:e + 1, :], 0)
        pend = pstart + padded
        pstart_v[...] = (pstart + BLOCK_M).astype(F32)
        carry_v[...] = jnp.zeros_like(carry_v)
        nb = pend[N_EXPERTS - 1:N_EXPERTS, :] >> BLOCK_SHIFT
        nb_ref[...] = nb

        blk = lax.broadcasted_iota(I32, (N_EXPERTS, BLOCK_LANES), 1)
        pend_w = jnp.concatenate([pend] * (BLOCK_LANES // LANES), axis=1)
        be = jnp.sum((pend_w <= blk * BLOCK_M).astype(I32), axis=0, keepdims=True)
        be = jnp.minimum(be, N_EXPERTS - 1)
        used = blk[0:1, :] < jnp.concatenate([nb] * (BLOCK_LANES // LANES), axis=1)
        last = jnp.max(jnp.where(used, be, 0), axis=1, keepdims=True)
        be_ref[...] = jnp.where(used, be, last)

        def pad_expert(e, first):
            n = cnts_ref[e]
            n_pad = (((n + (BLOCK_M - 1)) >> BLOCK_SHIFT) << BLOCK_SHIFT) - n

            def pad_row(r, _):
                inv_ref[BLOCK_M + first + n + r] = _pad_entry(e * BLOCK_M + r)
                return 0

            lax.fori_loop(0, n_pad, pad_row, 0)
            return first + n + n_pad

        lax.fori_loop(0, N_EXPERTS, pad_expert, 0)

        def lead_row(r, _):
            inv_ref[r] = _pad_entry(N_EXPERTS * BLOCK_M + r)
            return 0

        lax.fori_loop(0, BLOCK_M, lead_row, 0)

    def stage_copy(grp):
        return pltpu.make_async_copy(stage_v.at[grp % 3],
                                     stage_s.at[pl.ds(pl.multiple_of((grp % 2) * GROUP_ROWS, GROUP_ROWS), GROUP_ROWS)],
                                     sem.at[grp % 2])

    @pl.when((g >= 1) & (g <= N_GROUPS_RANK))
    def _():
        stage_copy(g - 1).start()

    @pl.when(g >= RANK_LAG)
    def _():
        stage_copy(g - RANK_LAG).wait()

    upper = (lax.broadcasted_iota(I32, (LANES, LANES), 0)
             < lax.broadcasted_iota(I32, (LANES, LANES), 1)).astype(BF16)

    def rank_chunk(i):
        ids = idx_ref[pl.ds(g * GROUP_ROWS + i, 1), :]
        hit = sub == ids
        hit_f = hit.astype(F32)
        before = _dot(hit_f.astype(BF16), upper)
        carry = carry_v[...]
        pos = jnp.sum(jnp.where(hit, before + carry + pstart_v[...], 0.0), axis=0, keepdims=True)
        stage_v[g % 3, pl.ds(i, 1), :] = pos.astype(I32)
        carry_v[...] = carry + jnp.sum(hit_f, axis=1, keepdims=True)

    def write_chunk(i):
        row = (g - RANK_LAG) * GROUP_ROWS + i
        k = row // N_CHUNKS
        tok0 = (row - k * N_CHUNKS) * LANES
        val0 = _pack_row(tok0, k * SLAB + tok0)
        srow = ((g - RANK_LAG) % 2) * GROUP_ROWS + i
        for l0 in range(0, LANES, INVERT_GROUP):
            dests = [stage_s[srow, l0 + l] for l in range(INVERT_GROUP)]
            for l, d in enumerate(dests):
                inv_ref[d] = val0 + (l0 + l) * _pack_row(1, 1)

    @pl.when(g < RANK_LAG)
    def _():
        for i in range(GROUP_ROWS):
            rank_chunk(i)

    @pl.when((g >= RANK_LAG) & (g < N_GROUPS_RANK))
    def _():
        for i in range(GROUP_ROWS):
            rank_chunk(i)
            write_chunk(i)

    @pl.when(g >= N_GROUPS_RANK)
    def _():
        for i in range(GROUP_ROWS):
            pl.when((g - RANK_LAG) * GROUP_ROWS + i < TOP_K * N_CHUNKS)(functools.partial(write_chunk, i))


def _route_tables(cnt, idx_rows, cnt_col):
    smem = pl.BlockSpec(memory_space=pltpu.SMEM)
    vmem = pl.BlockSpec(memory_space=pltpu.VMEM)
    return pl.pallas_call(
        _route_tables_kernel,
        grid=(N_GROUPS_RANK + RANK_LAG,),
        in_specs=[smem, vmem, vmem],
        out_specs=[smem, vmem, vmem],
        out_shape=[
            jax.ShapeDtypeStruct(((N_BLOCKS + 1) * BLOCK_M,), I32),
            jax.ShapeDtypeStruct((1, BLOCK_LANES), I32),
            jax.ShapeDtypeStruct((1, LANES), I32),
        ],
        scratch_shapes=[
            pltpu.VMEM((N_EXPERTS, LANES), F32),
            pltpu.VMEM((N_EXPERTS, LANES), F32),
            pltpu.VMEM((3, GROUP_ROWS, LANES), I32),
            pltpu.SMEM((2 * GROUP_ROWS, LANES), I32),
            pltpu.SemaphoreType.DMA((2,)),
        ],
        compiler_params=pltpu.CompilerParams(dimension_semantics=("arbitrary",)),
        name="route_tables",
    )(cnt, idx_rows, cnt_col)


X_SLOTS = 4


def _experts_kernel(inv_ref, be_ref, nb_ref, hn_hbm, wup_hbm, bup_hbm, wdn_hbm, bdn_hbm, y_hbm,
                    xbuf, obuf, actbuf, wup_f, wdn_f, wup_b, wdn_b, bup_v, bdn_v, sem_in, sem_out, sem_w):
    j = pl.program_id(0)
    nb = nb_ref[0]

    def token_rows(t):
        if isinstance(t, int):
            return pl.ds(t * TOKEN_ROWS, TOKEN_ROWS)
        return pl.ds(pl.multiple_of(t * TOKEN_ROWS, TOKEN_ROWS), TOKEN_ROWS)

    def gather_rows(blk, s):
        base = (blk + 1) * BLOCK_M
        for r in range(BLOCK_M):
            tok = inv_ref[base + r] & ((1 << SRC_BITS) - 1)
            pltpu.make_async_copy(hn_hbm.at[token_rows(tok)], xbuf.at[s, token_rows(r)], sem_in.at[s]).start()

    def scatter_rows(blk, s):
        base = (blk + 1) * BLOCK_M
        for r in range(BLOCK_M):
            row = lax.shift_right_logical(inv_ref[base + r], SRC_BITS)
            pltpu.make_async_copy(obuf.at[s, token_rows(r)], y_hbm.at[token_rows(row)], sem_out.at[s]).start()

    def wait_gather(s):
        pltpu.make_async_copy(hn_hbm.at[pl.ds(0, BLOCK_M * TOKEN_ROWS)], xbuf.at[s], sem_in.at[s]).wait()

    def wait_scatter(s):
        pltpu.make_async_copy(obuf.at[s], y_hbm.at[pl.ds(0, BLOCK_M * TOKEN_ROWS)], sem_out.at[s]).wait()

    @pl.when(j == 0)
    def _():
        gather_rows(0, 0)
        gather_rows(jnp.minimum(1, nb - 1), 1)
        obuf[1] = jnp.zeros(obuf.shape[1:], F32)

    def weight_copies(e):
        return (pltpu.make_async_copy(wup_hbm.at[e], wup_f, sem_w.at[0]),
                pltpu.make_async_copy(wdn_hbm.at[e], wdn_f, sem_w.at[1]))

    def bias_copies():
        return (pltpu.make_async_copy(bup_hbm, bup_v, sem_w.at[2]),
                pltpu.make_async_copy(bdn_hbm, bdn_v, sem_w.at[3]))

    expert = be_ref[j]

    @pl.when(j == 0)
    def _():
        for c in weight_copies(expert) + bias_copies():
            c.start()

    @pl.when((j < nb) & ((j == 0) | (expert != be_ref[jnp.maximum(j - 1, 0)])))
    def _():
        for c in weight_copies(expert):
            c.wait()

        @pl.when(j == 0)
        def _():
            for c in bias_copies():
                c.wait()

        wup_b[...] = wup_f[...].astype(BF16)
        wdn_b[...] = wdn_f[...].astype(BF16)
        nxt = lax.while_loop(
            lambda jj: (jj < nb) & (be_ref[jnp.minimum(jj, N_BLOCKS - 1)] == expert), lambda jj: jj + 1, j + 1)

        @pl.when(nxt < nb)
        def _():
            for c in weight_copies(be_ref[jnp.minimum(nxt, N_BLOCKS - 1)]):
                c.start()

    def step(x_slot):
        o_slot = x_slot % 2

        wait_gather(x_slot)
        scatter_rows(j - 1, 1 - o_slot)
        x = _load_token_tiles(xbuf.at[x_slot]).astype(BF16)
        hmid = _dot(x, wup_b[...]) + bup_v[expert]
        h_glu = jnp.minimum(hmid[:, 0:D_FF], SWIGLU_LIMIT)
        h_lin = jnp.clip(hmid[:, D_FF:2 * D_FF], -SWIGLU_LIMIT, SWIGLU_LIMIT)
        actbuf[...] = (h_glu * jax.nn.sigmoid(SWIGLU_ALPHA * h_glu) * (h_lin + 1.0)).astype(BF16)

        @pl.when(j >= 1)
        def _():
            wait_scatter(o_slot)

        gather_rows(jnp.minimum(j + 2, nb - 1), (x_slot + 2) % X_SLOTS)
        _store_token_tiles(obuf.at[o_slot], _dot(actbuf[...], wdn_b[...]) + bdn_v[expert])

        @pl.when(j == nb - 1)
        def _():
            scatter_rows(j, o_slot)
            wait_scatter(1 - o_slot)
            wait_scatter(o_slot)
            wait_gather((x_slot + 1) % X_SLOTS)
            wait_gather((x_slot + 2) % X_SLOTS)

    for x_slot in range(X_SLOTS):
        pl.when((j < nb) & (j % X_SLOTS == x_slot))(functools.partial(step, x_slot))


def _experts(inv, be, nb, hn_all, w_up, b_up, w_down, b_down):
    grid_spec = pltpu.PrefetchScalarGridSpec(
        num_scalar_prefetch=3,
        grid=(N_BLOCKS,),
        in_specs=[pl.BlockSpec(memory_space=pl.ANY)] * 5,
        out_specs=pl.BlockSpec(memory_space=pl.ANY),
        scratch_shapes=[
            pltpu.VMEM((X_SLOTS, BLOCK_M * TOKEN_ROWS, LANES), F32),
            pltpu.VMEM((2, BLOCK_M * TOKEN_ROWS, LANES), F32),
            pltpu.VMEM((BLOCK_M, D_FF), BF16),
            pltpu.VMEM((D_MODEL, 2 * D_FF), F32),
            pltpu.VMEM((D_FF, D_MODEL), F32),
            pltpu.VMEM((D_MODEL, 2 * D_FF), BF16),
            pltpu.VMEM((D_FF, D_MODEL), BF16),
            pltpu.VMEM((N_EXPERTS, 1, 2 * D_FF), F32),
            pltpu.VMEM((N_EXPERTS, 1, D_MODEL), F32),
            pltpu.SemaphoreType.DMA((X_SLOTS,)),
            pltpu.SemaphoreType.DMA((2,)),
            pltpu.SemaphoreType.DMA((4,)),
        ],
    )
    return pl.pallas_call(
        _experts_kernel,
        grid_spec=grid_spec,
        out_shape=jax.ShapeDtypeStruct((TOP_K * SLAB * TOKEN_ROWS, LANES), F32),
        compiler_params=pltpu.CompilerParams(
            dimension_semantics=("arbitrary",), vmem_limit_bytes=VMEM_LIMIT),
        name="experts",
    )(inv, be, nb, hn_all, w_up, b_up, w_down, b_down)


def _combine_kernel(ys_ref, gate_ref, h_ref, nfin_ref, out_ref):
    gates = gate_ref[...]
    acc = h_ref[...]
    for k in range(TOP_K):
        acc = acc + gates[:, k:k + 1] * _load_token_tiles(ys_ref.at[k])
    out_ref[...] = _rmsnorm(acc, nfin_ref[...])


def _combine(ys, gates, h_all, nfin, rows, tile, first_block, name):
    return pl.pallas_call(
        _combine_kernel,
        grid=(rows // tile,),
        in_specs=[
            pl.BlockSpec((TOP_K, tile * TOKEN_ROWS, LANES), lambda i: (0, first_block + i, 0)),
            pl.BlockSpec((tile, SUBLANES), lambda i: (first_block + i, 0)),
            pl.BlockSpec((tile, D_MODEL), lambda i: (first_block + i, 0)),
            pl.BlockSpec((1, D_MODEL), lambda i: (0, 0)),
        ],
        out_specs=pl.BlockSpec((tile, D_MODEL), lambda i: (i, 0)),
        out_shape=jax.ShapeDtypeStruct((rows, D_MODEL), F32),
        compiler_params=pltpu.CompilerParams(
            dimension_semantics=("arbitrary",), vmem_limit_bytes=VMEM_LIMIT),
        name=name,
    )(ys, gates, h_all, nfin)


def _rope_tables(positions):
    half = RET_HD // 2
    inv = ROPE_BASE ** (-jnp.arange(half, dtype=F32) / half)
    ang = positions.astype(F32)[:, None] * inv[None, :]
    cos, sin = jnp.cos(ang), jnp.sin(ang)
    return jnp.concatenate([cos, cos], axis=-1), jnp.concatenate([-sin, sin], axis=-1)


def _decay_tables(log_gamma, c):
    pos = jnp.arange(c, dtype=F32)
    diff = pos[:, None] - pos[None, :]
    causal = diff >= 0
    lg = log_gamma[:, None, None]
    decay = jnp.where(causal, jnp.exp(lg * jnp.where(causal, diff, 0.0)), 0.0)
    xi = jnp.exp(log_gamma[:, None] * (pos[None, :] + 1.0))
    zeta = jnp.exp(log_gamma[:, None] * (c - 1.0 - pos[None, :]))
    wide = lambda t: jnp.broadcast_to(t[..., None], t.shape + (RET_HD,))
    gl = jnp.broadcast_to(jnp.exp(log_gamma * c)[:, None, None], (RET_HEADS, 1, RET_HD))
    return decay, wide(xi), wide(zeta), gl


def kernel(x_prompt, x_sample, state_conv, state_ret, norm_mix, w_in, conv_w, ret_norm, w_out, norm_ffn,
           w_router, b_router, w_up, b_up, w_down, b_down, norm_final):
    assert norm_mix.shape[0] == 1, "single trunk layer"
    nmix = norm_mix[0][None, :]
    nffn = norm_ffn[0][None, :]
    nfin = norm_final[None, :]
    rnorm = ret_norm[0][None, :]
    win = w_in[0].astype(BF16)
    wout = w_out[0].astype(BF16)
    wr = w_router[0]
    wrhi = wr.astype(BF16)
    wrlo = (wr - wrhi.astype(F32)).astype(BF16)
    br = b_router[0][None, :]
    convw = conv_w[0]

    log_gamma = jnp.log(1.0 - 2.0 ** (-5.0 - jnp.arange(RET_HEADS, dtype=F32)))
    decay, xi, zeta, gl = _decay_tables(log_gamma, TILE_L)
    cos_p, sin_p = _rope_tables(jnp.arange(SEQ, dtype=jnp.int32))
    cos_s, sin_s = _rope_tables(PAST_LEN + jnp.arange(1, dtype=jnp.int32))
    gamma1 = jnp.broadcast_to(jnp.exp(log_gamma)[:, None, None], (RET_HEADS, 1, RET_HD))

    h_s, hn_s, idx_s, gate_s, cnt_s, conv_s, ret_s = _mixer_sample(
        x_sample.reshape(DEC_BATCH, D_MODEL),
        state_conv[0].reshape(DEC_BATCH, (CONV_K - 1) * CONV_W),
        state_ret[0], nmix, win, convw, cos_s, sin_s, gamma1, rnorm, wout, nffn, wrhi, wrlo, br)

    h_all, hn_all, idx_all, gate_all, cnt, conv_p, ret_p = _mixer_prompt(
        x_prompt, nmix, win, convw, cos_p, sin_p, decay, xi, zeta, gl, rnorm, wout, nffn, wrhi, wrlo, br,
        h_s, hn_s, idx_s, gate_s, cnt_s)

    idx_rows = jnp.concatenate(
        [idx_all[0:T_ALL, 0:TOP_K].T.reshape(TOP_K * N_CHUNKS, LANES),
         jnp.full((DEST_ROWS - TOP_K * N_CHUNKS, LANES), N_EXPERTS, I32)], axis=0)
    cnt_i = cnt[0].astype(I32)
    cnt_col = jnp.broadcast_to(cnt_i[:, None], (N_EXPERTS, LANES))
    inv, be, nb = _route_tables(cnt_i, idx_rows, cnt_col)

    ys = _experts(inv, be[0, 0:N_BLOCKS], nb[0, 0:1], hn_all,
                  w_up[0], b_up[0][:, None, :], w_down[0], b_down[0][:, None, :])
    ys = ys.reshape(TOP_K, SLAB * TOKEN_ROWS, LANES)

    y_prompt = _combine(ys, gate_all, h_all, nfin, T_PROMPT, FINAL_TILE, 0, "combine_prompt")
    y_sample = _combine(ys, gate_all, h_all, nfin, DEC_BATCH, DEC_BATCH, T_PROMPT // DEC_BATCH,
                        "combine_sample")

    return (y_prompt.reshape(BATCH, SEQ, D_MODEL),
            y_sample.reshape(DEC_BATCH, 1, D_MODEL),
            conv_p,
            ret_p,
            conv_s.reshape(1, DEC_BATCH, CONV_K - 1, CONV_W),
            ret_s.reshape(1, DEC_BATCH, RET_HEADS, RET_HD, RET_HD))
```
